```python
import math
import jax
import jax.numpy as jnp
from jax import lax
import numpy as np

D_MODEL = 1024
BATCH = 32
SEQ = 2048
DEPTH = 2

GRID_W = 64
CTX_LEN = 256
NORM_EPS = 1e-6

DN_HEADS = 4
DN_DK = 128
DN_DV = 128
DN_CONV = 3
DN_CHUNK = 64
SC_WIDTH = 512
SC_CONV = 3
ATTN_HEADS = 8
ATTN_KV_HEADS = 2
ATTN_GROUP = ATTN_HEADS // ATTN_KV_HEADS
ATTN_HD = 128
ATTN_BLOCK = 128
ROPE_THETA = 10000.0
N_EXPERTS = 64
TOP_K = 8
N_GROUPS = 8
TOPK_GROUPS = 4
EXPERT_FF = 256
SHARED_FF = 256
ROUTED_SCALE = 2.5
MOE_BLOCK = 256

DN_QK_W = DN_HEADS * DN_DK
DN_V_W = DN_HEADS * DN_DV
ATTN_Q_W = ATTN_HEADS * ATTN_HD
ATTN_KV_W = ATTN_KV_HEADS * ATTN_HD
N_BRANCH = 3
KV_SIZES = (DN_QK_W + DN_V_W, 2 * DN_HEADS, 2 * DN_HEADS, ATTN_KV_W, ATTN_KV_W)
REST_SIZES = (DN_QK_W, DN_V_W, SC_WIDTH, SC_WIDTH, SC_WIDTH, ATTN_Q_W, N_BRANCH * D_MODEL)
KV_COLS = sum(KV_SIZES)
IN_COLS = KV_COLS + sum(REST_SIZES)

kernel_name = 'hybrid_deltanet_shortconv_axialgqa_moe_dit'


def _split(x, sizes):
    cuts = [int(v) for v in np.cumsum(sizes)[:-1]]
    return jnp.split(x, cuts, axis=-1)


def rmsnorm(x, w):
    xf = x.astype(jnp.float32)
    y = xf * lax.rsqrt(jnp.mean(xf * xf, axis=-1, keepdims=True) + NORM_EPS)
    return (y * w.astype(jnp.float32)).astype(x.dtype)


def l2norm(x):
    return x * lax.rsqrt(jnp.sum(x * x, axis=-1, keepdims=True) + NORM_EPS)


def dwconv_centred(x, w):
    pad = w.shape[0] // 2
    return lax.conv_general_dilated(x, w.astype(x.dtype)[:, None, :], window_strides=(1,),
                                    padding=[(pad, pad)], dimension_numbers=('NWC', 'WIO', 'NWC'),
                                    feature_group_count=x.shape[-1])


def axial_rope_tables(rows):
    row_pos = jnp.repeat(jnp.arange(rows, dtype=jnp.float32), GRID_W)
    col_pos = jnp.tile(jnp.arange(GRID_W, dtype=jnp.float32), rows)
    axis_dim = ATTN_HD // 2
    inv_freq = ROPE_THETA ** (-jnp.arange(0, axis_dim, 2, dtype=jnp.float32) / axis_dim)
    ang = jnp.concatenate([row_pos[:, None] * inv_freq, col_pos[:, None] * inv_freq], axis=-1)
    return jnp.cos(ang), jnp.sin(ang)


def apply_rope(x, cos, sin):
    xf = x.astype(jnp.float32)
    x1, x2 = xf[..., 0::2], xf[..., 1::2]
    cs, sn = cos[None, :, None, :], sin[None, :, None, :]
    out = jnp.stack([x1 * cs - x2 * sn, x1 * sn + x2 * cs], axis=-1)
    return out.reshape(x.shape).astype(x.dtype)


def _rev(t, direction):
    return jnp.flip(t, axis=1) if direction == 1 else t


def _to_chunks(t):
    B, L = t.shape[:2]
    t = t.reshape((B, L // DN_CHUNK, DN_CHUNK) + t.shape[2:])
    return jnp.moveaxis(t, 3, 1)


def gated_delta_chunked(q, k, v, g, beta, s0):
    B, L, H, _ = k.shape
    kc, vc, gc, bc = (_to_chunks(t) for t in (k, v, g, beta))
    gcum = jnp.cumsum(gc, axis=-1)
    incl = jnp.tril(jnp.ones((DN_CHUNK, DN_CHUNK), bool))
    decay = jnp.exp(jnp.where(incl, gcum[..., :, None] - gcum[..., None, :], -jnp.inf))
    strict = jnp.tril(jnp.ones((DN_CHUNK, DN_CHUNK), jnp.float32), -1)
    kb = kc * bc[..., None]
    lmat = jnp.einsum('bhnid,bhnjd->bhnij', kb, kc) * decay * strict
    eye = jnp.eye(DN_CHUNK, dtype=jnp.float32)
    rhs = jnp.concatenate([vc * bc[..., None], kb * jnp.exp(gcum)[..., None]], axis=-1)
    sol = lax.linalg.triangular_solve(lmat + eye, rhs, left_side=True, lower=True, unit_diagonal=True)
    u0, wk = sol[..., :DN_DV], sol[..., DN_DV:]
    kdec = kc * jnp.exp(gcum[..., -1:] - gcum)[..., None]
    chunk_decay = jnp.exp(gcum[..., -1])
    with_out = q is not None
    xs = [u0, wk, kdec, chunk_decay]
    if with_out:
        qc = _to_chunks(q)
        xs += [jnp.einsum('bhnid,bhnjd->bhnij', qc, kc) * decay, qc * jnp.exp(gcum)[..., None]]
    xs = [jnp.moveaxis(t, 2, 0) for t in xs]

    def step(s, inp):
        u = inp[0] - jnp.einsum('bhcd,bhde->bhce', inp[1], s)
        s_new = s * inp[3][..., None, None] + jnp.einsum('bhcd,bhce->bhde', inp[2], u)
        if with_out:
            o = jnp.einsum('bhcd,bhde->bhce', inp[5], s) + jnp.einsum('bhcj,bhje->bhce', inp[4], u)
            return s_new, o
        return s_new, None

    s_fin, o = lax.scan(step, s0, xs)
    if not with_out:
        return s_fin, None
    o = jnp.moveaxis(jnp.moveaxis(o, 0, 2), 1, 3).reshape(B, L, H, DN_DV)
    return s_fin, o


def attend(q, k, v):
    s = jnp.einsum('bqhgd,bkhd->bhgqk', q, k, preferred_element_type=jnp.float32) * (ATTN_HD ** -0.5)
    p = jax.nn.softmax(s, axis=-1).astype(v.dtype)
    return jnp.einsum('bhgqk,bkhd->bqhgd', p, v)


def attention_latent(q, k, v, k_ctx, v_ctx):
    B, L = q.shape[:2]
    k_all = jnp.concatenate([k_ctx, k], axis=1)
    v_all = jnp.concatenate([v_ctx, v], axis=1)
    qb = q.reshape(B, L // ATTN_BLOCK, ATTN_BLOCK, ATTN_KV_HEADS, ATTN_GROUP, ATTN_HD)
    o = lax.map(lambda blk: attend(blk, k_all, v_all), jnp.moveaxis(qb, 1, 0))
    return jnp.moveaxis(o, 0, 1).reshape(B, L, ATTN_Q_W)


def kv_side(kv, w_conv_kv, dn_a_log, dn_dt_bias, k_norm):
    B, L = kv.shape[:2]
    dn_kv, dn_a, dn_b, at_k, at_v = _split(kv, KV_SIZES)
    dn_kv = jax.nn.silu(dwconv_centred(dn_kv, w_conv_kv)).astype(jnp.float32)
    dn_k = l2norm(dn_kv[..., :DN_QK_W].reshape(B, L, DN_HEADS, DN_DK))
    dn_v = dn_kv[..., DN_QK_W:].reshape(B, L, DN_HEADS, DN_DV)
    a = dn_a.astype(jnp.float32).reshape(B, L, 2, DN_HEADS)
    dn_g = -jnp.exp(dn_a_log.astype(jnp.float32)) * jax.nn.softplus(a + dn_dt_bias.astype(jnp.float32))
    dn_beta = jax.nn.sigmoid(dn_b.astype(jnp.float32).reshape(B, L, 2, DN_HEADS))
    at_k = rmsnorm(at_k.reshape(B, L, ATTN_KV_HEADS, ATTN_HD), k_norm)
    at_v = at_v.reshape(B, L, ATTN_KV_HEADS, ATTN_HD)
    return dn_k, dn_v, dn_g, dn_beta, at_k, at_v


def rest_side(rest, w_conv_q, q_norm):
    B, L = rest.shape[:2]
    dn_q, dn_z, sc_u, sc_b, sc_c, at_q, gates = _split(rest, REST_SIZES)
    dn_q = jax.nn.silu(dwconv_centred(dn_q, w_conv_q)).astype(jnp.float32)
    dn_q = l2norm(dn_q.reshape(B, L, DN_HEADS, DN_DK)) * (DN_DK ** -0.5)
    at_q = rmsnorm(at_q.reshape(B, L, ATTN_HEADS, ATTN_HD), q_norm)
    return dn_q, dn_z, sc_u, sc_b, sc_c, at_q, gates


def merge_branches(o_dn, dn_z, sc_u, sc_b, sc_c, o_attn, gates, dn_norm, w_sc_conv,
                   w_proj_dn, w_proj_sc, w_proj_attn, w_out):
    B, L = dn_z.shape[:2]
    z = dn_z.reshape(B, L, DN_HEADS, DN_DV)
    y_dn = (rmsnorm(o_dn, dn_norm).astype(z.dtype) * jax.nn.silu(z)).reshape(B, L, DN_V_W) @ w_proj_dn
    y_sc = (sc_b * dwconv_centred(sc_c * sc_u, w_sc_conv)) @ w_proj_sc
    y_at = o_attn @ w_proj_attn
    g_dn, g_sc, g_at = jnp.split(jax.nn.sigmoid(gates), N_BRANCH, axis=-1)
    return (g_dn * y_dn + g_sc * y_sc + g_at * y_at) @ w_out


def mix_sublayer(h_l, h_c, w_in, w_dn_conv_q, w_dn_conv_kv, dn_a_log, dn_dt_bias, dn_norm, w_sc_conv,
                 q_norm, k_norm, w_proj_dn, w_proj_sc, w_proj_attn, w_out, cos, sin, ctx_out):
    B, Lc = h_c.shape[:2]
    p_l = h_l @ w_in
    p_c = h_c @ (w_in if ctx_out else w_in[:, :KV_COLS])
    dn_kl, dn_vl, dn_gl, dn_bl, at_kl, at_vl = kv_side(p_l[..., :KV_COLS], w_dn_conv_kv, dn_a_log, dn_dt_bias, k_norm)
    dn_kc, dn_vc, dn_gc, dn_bc, at_kc, at_vc = kv_side(p_c[..., :KV_COLS], w_dn_conv_kv, dn_a_log, dn_dt_bias, k_norm)
    dn_ql, z_l, scu_l, scb_l, scc_l, at_ql, gates_l = rest_side(p_l[..., KV_COLS:], w_dn_conv_q, q_norm)
    at_ql = apply_rope(at_ql, cos, sin)
    at_kl = apply_rope(at_kl, cos, sin)
    dn_qc = None
    if ctx_out:
        dn_qc, z_c, scu_c, scb_c, scc_c, at_qc, gates_c = rest_side(p_c[..., KV_COLS:], w_dn_conv_q, q_norm)
    s_zero = jnp.zeros((B, DN_HEADS, DN_DK, DN_DV), jnp.float32)
    o_dn_l, o_dn_c = [], []
    for d in range(2):
        q_ctx_d = None if dn_qc is None else _rev(dn_qc, d)
        s_ctx, o_c = gated_delta_chunked(q_ctx_d, _rev(dn_kc, d), _rev(dn_vc, d),
                                         _rev(dn_gc[:, :, d], d), _rev(dn_bc[:, :, d], d), s_zero)
        _, o_l = gated_delta_chunked(_rev(dn_ql, d), _rev(dn_kl, d), _rev(dn_vl, d),
                                     _rev(dn_gl[:, :, d], d), _rev(dn_bl[:, :, d], d), s_ctx)
        o_dn_l.append(_rev(o_l, d))
        if ctx_out:
            o_dn_c.append(_rev(o_c, d))
    o_at_l = attention_latent(at_ql, at_kl, at_vl, at_kc, at_vc)
    y_l = merge_branches(o_dn_l[0] + o_dn_l[1], z_l, scu_l, scb_l, scc_l, o_at_l, gates_l, dn_norm, w_sc_conv,
                         w_proj_dn, w_proj_sc, w_proj_attn, w_out)
    if not ctx_out:
        return y_l, None
    q_c = at_qc.reshape(B, Lc, ATTN_KV_HEADS, ATTN_GROUP, ATTN_HD)
    o_at_c = attend(q_c, at_kc, at_vc).reshape(B, Lc, ATTN_Q_W)
    y_c = merge_branches(o_dn_c[0] + o_dn_c[1], z_c, scu_c, scb_c, scc_c, o_at_c, gates_c, dn_norm, w_sc_conv,
                         w_proj_dn, w_proj_sc, w_proj_attn, w_out)
    return y_l, y_c


def swiglu(x, wg, wu, wd):
    return (jax.nn.silu(x @ wg) * (x @ wu)) @ wd


def route(h, w_router, b_router):
    T = h.shape[0]
    scores = jax.nn.sigmoid((h @ w_router).astype(jnp.float32))
    sel = scores + b_router.astype(jnp.float32)
    grp = sel.reshape(T, N_GROUPS, N_EXPERTS // N_GROUPS)
    grp_score = jnp.sum(lax.top_k(grp, 2)[0], axis=-1)
    _, gidx = lax.top_k(grp_score, TOPK_GROUPS)
    gmask = jnp.any(gidx[:, :, None] == jnp.arange(N_GROUPS)[None, None, :], axis=1)
    emask = jnp.repeat(gmask, N_EXPERTS // N_GROUPS, axis=1)
    _, idx = lax.top_k(jnp.where(emask, sel, -jnp.inf), TOP_K)
    w = jnp.take_along_axis(scores, idx, axis=1)
    w = w / (jnp.sum(w, axis=-1, keepdims=True) + 1e-20) * ROUTED_SCALE
    return idx, w


def moe_ffn(h, w_router, b_router, w_gate, w_up, w_down, ws_gate, ws_up, ws_down):
    T, D = h.shape
    idx, wts = route(h, w_router, b_router)
    n_assign = T * TOP_K
    n_blocks = -(-n_assign // MOE_BLOCK) + N_EXPERTS
    n_slots = n_blocks * MOE_BLOCK
    flat_e = idx.reshape(-1)
    flat_t = jnp.repeat(jnp.arange(T, dtype=jnp.int32), TOP_K)
    flat_w = wts.reshape(-1).astype(h.dtype)
    order = jnp.argsort(flat_e)
    e_sorted = flat_e[order]
    counts = jnp.bincount(flat_e, length=N_EXPERTS)
    padded = (counts + MOE_BLOCK - 1) // MOE_BLOCK * MOE_BLOCK
    pad_end = jnp.cumsum(padded)
    pad_start = pad_end - padded
    start = jnp.cumsum(counts) - counts
    dest = pad_start[e_sorted] + jnp.arange(n_assign) - start[e_sorted]
    slot_tok = jnp.full((n_slots,), T, jnp.int32).at[dest].set(flat_t[order])
    slot_w = jnp.zeros((n_slots,), h.dtype).at[dest].set(flat_w[order])
    block_e = jnp.minimum(jnp.searchsorted(pad_end, jnp.arange(n_blocks) * MOE_BLOCK, side='right'), N_EXPERTS - 1)
    h_pad = jnp.concatenate([h, jnp.zeros((1, D), h.dtype)], axis=0)

    def expert_block(acc, inp):
        tok, wt, e = inp
        y = swiglu(h_pad[tok], w_gate[e], w_up[e], w_down[e])
        return acc.at[tok].add(y * wt[:, None]), None

    acc, _ = lax.scan(expert_block, jnp.zeros((T + 1, D), h.dtype),
                      (slot_tok.reshape(n_blocks, MOE_BLOCK), slot_w.reshape(n_blocks, MOE_BLOCK), block_e))
    return acc[:T] + swiglu(h, ws_gate, ws_up, ws_down)


def setup_inputs(seed: int = 0) -> dict:
    key = jax.random.key(seed)
    ks = iter(jax.random.split(key, 40))
    f32 = jnp.float32
    D = D_MODEL

    def normal(shape, scale=1.0):
        return jax.random.normal(next(ks), shape, f32) * scale

    def gain(shape):
        return 1.0 + normal(shape, 0.05)

    dt = jnp.exp(jax.random.uniform(next(ks), (DEPTH, 2, DN_HEADS), f32, math.log(1e-3), math.log(1e-1)))
    a_init = jax.random.uniform(next(ks), (DEPTH, 2, DN_HEADS), f32, 1.0, 16.0)
    return {
        'x': normal((BATCH, SEQ, D)),
        'c': normal((BATCH, D)),
        'ctx': normal((BATCH, CTX_LEN, D)),
        'c_ctx': normal((D,)),
        'w_mod': normal((DEPTH, D, 6 * D), 0.5 * D ** -0.5),
        'b_mod': normal((DEPTH, 6 * D), 0.02),
        'mix_pre': gain((DEPTH, D)),
        'mix_post': gain((DEPTH, D)),
        'ffn_pre': gain((DEPTH, D)),
        'ffn_post': gain((DEPTH, D)),
        'w_in': normal((DEPTH, D, IN_COLS), D ** -0.5),
        'w_dn_conv_q': normal((DEPTH, DN_CONV, DN_QK_W), DN_CONV ** -0.5),
        'w_dn_conv_kv': normal((DEPTH, DN_CONV, DN_QK_W + DN_V_W), DN_CONV ** -0.5),
        'dn_a_log': jnp.log(a_init),
        'dn_dt_bias': dt + jnp.log(-jnp.expm1(-dt)),
        'dn_norm': gain((DEPTH, DN_DV)),
        'w_sc_conv': normal((DEPTH, SC_CONV, SC_WIDTH), SC_CONV ** -0.5),
        'q_norm': gain((DEPTH, ATTN_HD)),
        'k_norm': gain((DEPTH, ATTN_HD)),
        'w_proj_dn': normal((DEPTH, DN_V_W, D), DN_V_W ** -0.5),
        'w_proj_sc': normal((DEPTH, SC_WIDTH, D), SC_WIDTH ** -0.5),
        'w_proj_attn': normal((DEPTH, ATTN_Q_W, D), ATTN_Q_W ** -0.5),
        'w_out': normal((DEPTH, D, D), D ** -0.5),
        'w_router': normal((DEPTH, D, N_EXPERTS), D ** -0.5),
        'b_router': normal((DEPTH, N_EXPERTS), 0.01),
        'w_exp_gate': normal((DEPTH, N_EXPERTS, D, EXPERT_FF), D ** -0.5),
        'w_exp_up': normal((DEPTH, N_EXPERTS, D, EXPERT_FF), D ** -0.5),
        'w_exp_down': normal((DEPTH, N_EXPERTS, EXPERT_FF, D), EXPERT_FF ** -0.5),
        'w_sh_gate': normal((DEPTH, D, SHARED_FF), D ** -0.5),
        'w_sh_up': normal((DEPTH, D, SHARED_FF), D ** -0.5),
        'w_sh_down': normal((DEPTH, SHARED_FF, D), SHARED_FF ** -0.5),
    }


def reference(x, c, ctx, c_ctx, w_mod, b_mod, mix_pre, mix_post, ffn_pre, ffn_post, w_in, w_dn_conv_q,
              w_dn_conv_kv, dn_a_log, dn_dt_bias, dn_norm, w_sc_conv, q_norm, k_norm, w_proj_dn, w_proj_sc,
              w_proj_attn, w_out, w_router, b_router, w_exp_gate, w_exp_up, w_exp_down, w_sh_gate, w_sh_up,
              w_sh_down):
    B, L, D = x.shape
    Lc = ctx.shape[1]
    rows = L // GRID_W
    cos, sin = axial_rope_tables(rows)
    h_lat, h_ctx = x, ctx
    silu_c = jax.nn.silu(c)
    silu_cc = jax.nn.silu(c_ctx)
    for layer in range(DEPTH):
        ctx_out = layer < DEPTH - 1
        n_mod_c = 6 if ctx_out else 3
        mod_l = silu_c @ w_mod[layer] + b_mod[layer]
        mod_c = silu_cc @ w_mod[layer][:, :n_mod_c * D] + b_mod[layer][:n_mod_c * D]
        sh1, sc1, g1, sh2, sc2, g2 = (m[:, None, :] for m in jnp.split(mod_l, 6, axis=-1))
        mc = jnp.split(mod_c, n_mod_c)
        hl = rmsnorm(h_lat, mix_pre[layer]) * (1 + sc1) + sh1
        hc = rmsnorm(h_ctx, mix_pre[layer]) * (1 + mc[1]) + mc[0]
        y_l, y_c = mix_sublayer(hl, hc, w_in[layer], w_dn_conv_q[layer], w_dn_conv_kv[layer], dn_a_log[layer],
                                dn_dt_bias[layer], dn_norm[layer], w_sc_conv[layer], q_norm[layer], k_norm[layer],
                                w_proj_dn[layer], w_proj_sc[layer], w_proj_attn[layer], w_out[layer], cos, sin, ctx_out)
        h_lat = h_lat + g1 * rmsnorm(y_l, mix_post[layer])
        fl = rmsnorm(h_lat, ffn_pre[layer]) * (1 + sc2) + sh2
        moe_w = (w_router[layer], b_router[layer], w_exp_gate[layer], w_exp_up[layer], w_exp_down[layer],
                 w_sh_gate[layer], w_sh_up[layer], w_sh_down[layer])
        if ctx_out:
            h_ctx = h_ctx + mc[2] * rmsnorm(y_c, mix_post[layer])
            fc = rmsnorm(h_ctx, ffn_pre[layer]) * (1 + mc[4]) + mc[3]
            f = moe_ffn(jnp.concatenate([fl.reshape(B * L, D), fc.reshape(B * Lc, D)], axis=0), *moe_w)
            f_l = f[:B * L].reshape(B, L, D)
            h_ctx = h_ctx + mc[5] * rmsnorm(f[B * L:].reshape(B, Lc, D), ffn_post[layer])
        else:
            f_l = moe_ffn(fl.reshape(B * L, D), *moe_w).reshape(B, L, D)
        h_lat = h_lat + g2 * rmsnorm(f_l, ffn_post[layer])
    return h_lat
```

```python
import functools
import math

import jax
import jax.numpy as jnp
import numpy as np
from jax import lax
from jax.experimental import pallas as pl
from jax.experimental.pallas import tpu as pltpu

F32 = jnp.float32
BF16 = jnp.bfloat16

GRID_W = 64
NORM_EPS = 1e-6
DN_HEADS = 4
DN_DK = 128
DN_DV = 128
DN_CHUNK = 64
SC_WIDTH = 512
ATTN_HEADS = 8
ATTN_KV_HEADS = 2
ATTN_GROUP = ATTN_HEADS // ATTN_KV_HEADS
ATTN_HD = 128
ROPE_THETA = 10000.0
N_EXPERTS = 64
TOP_K = 8
N_GROUPS = 8
TOPK_GROUPS = 4
ROUTED_SCALE = 2.5
MOE_BLOCK = 256
N_BRANCH = 3

DN_QK_W = DN_HEADS * DN_DK
DN_V_W = DN_HEADS * DN_DV
ATTN_Q_W = ATTN_HEADS * ATTN_HD
ATTN_KV_W = ATTN_KV_HEADS * ATTN_HD
LANES = 128


def _mm_kernel(x_ref, w_ref, o_ref):
    o_ref[...] = jnp.dot(x_ref[...], w_ref[...], preferred_element_type=F32).astype(o_ref.dtype)


def _pick_tile(n, want):
    t = min(n, want)
    while n % t:
        t //= 2
    return t


def matmul(x, w, out_dtype, tm=512, tn=1024):
    M, K = x.shape
    N = w.shape[1]
    tm = _pick_tile(M, tm)
    tn = _pick_tile(N, tn)
    return pl.pallas_call(
        _mm_kernel,
        grid=(N // tn, M // tm),
        in_specs=[pl.BlockSpec((tm, K), lambda j, i: (i, 0)),
                  pl.BlockSpec((K, tn), lambda j, i: (0, j))],
        out_specs=pl.BlockSpec((tm, tn), lambda j, i: (i, j)),
        out_shape=jax.ShapeDtypeStruct((M, N), out_dtype),
        name="matmul",
    )(x.astype(BF16), w.astype(BF16))


def _attn_kernel(q_ref, k_ref, v_ref, o_ref):
    q = q_ref[0]
    s = lax.dot_general(q, k_ref[0], (((1,), (1,)), ((), ())), preferred_element_type=F32)
    m = jnp.max(s, axis=-1, keepdims=True)
    p = jnp.exp(s - m)
    l = jnp.sum(p, axis=-1, keepdims=True)
    o = jnp.dot(p.astype(BF16), v_ref[0], preferred_element_type=F32)
    o_ref[0] = (o / l).astype(o_ref.dtype)


def attention(q, k, v, tq=512):
    B, Lq, _ = q.shape
    Lk = k.shape[1]
    tq = _pick_tile(Lq, tq)
    return pl.pallas_call(
        _attn_kernel,
        grid=(B, ATTN_KV_HEADS, ATTN_GROUP, Lq // tq),
        in_specs=[pl.BlockSpec((1, tq, ATTN_HD), lambda b, h, g, i: (b, i, h * ATTN_GROUP + g)),
                  pl.BlockSpec((1, Lk, ATTN_HD), lambda b, h, g, i: (b, 0, h)),
                  pl.BlockSpec((1, Lk, ATTN_HD), lambda b, h, g, i: (b, 0, h))],
        out_specs=pl.BlockSpec((1, tq, ATTN_HD), lambda b, h, g, i: (b, i, h * ATTN_GROUP + g)),
        out_shape=jax.ShapeDtypeStruct(q.shape, BF16),
        name="attention",
    )(q.astype(BF16), k.astype(BF16), v.astype(BF16))


def _expert_kernel(be_ref, x_ref, wg_ref, wu_ref, wd_ref, o_ref):
    del be_ref
    x = x_ref[...]
    g = jnp.dot(x, wg_ref[0], preferred_element_type=F32)
    u = jnp.dot(x, wu_ref[0], preferred_element_type=F32)
    a = (g * jax.nn.sigmoid(g) * u).astype(BF16)
    o_ref[...] = jnp.dot(a, wd_ref[0], preferred_element_type=F32).astype(o_ref.dtype)


def expert_blocks(x_sorted, block_e, wg, wu, wd, tm=MOE_BLOCK):
    n_rows, D = x_sorted.shape
    F = wg.shape[-1]
    n_blocks = n_rows // tm
    grid_spec = pltpu.PrefetchScalarGridSpec(
        num_scalar_prefetch=1,
        grid=(n_blocks,),
        in_specs=[pl.BlockSpec((tm, D), lambda i, be: (i, 0)),
                  pl.BlockSpec((1, D, F), lambda i, be: (be[i], 0, 0)),
                  pl.BlockSpec((1, D, F), lambda i, be: (be[i], 0, 0)),
                  pl.BlockSpec((1, F, D), lambda i, be: (be[i], 0, 0))],
        out_specs=pl.BlockSpec((tm, D), lambda i, be: (i, 0)),
    )
    return pl.pallas_call(
        _expert_kernel,
        grid_spec=grid_spec,
        out_shape=jax.ShapeDtypeStruct((n_rows, D), BF16),
        name="expert_blocks",
    )(block_e.astype(jnp.int32), x_sorted, wg, wu, wd)


def _split(x, sizes):
    cuts = [int(v) for v in np.cumsum(sizes)[:-1]]
    return jnp.split(x, cuts, axis=-1)


def rmsnorm(x, w):
    xf = x.astype(F32)
    y = xf * lax.rsqrt(jnp.mean(xf * xf, axis=-1, keepdims=True) + NORM_EPS)
    return y * w.astype(F32)


def l2norm(x):
    return x * lax.rsqrt(jnp.sum(x * x, axis=-1, keepdims=True) + NORM_EPS)


def dwconv3(x, w):
    xf = x.astype(F32)
    prev = jnp.pad(xf, ((0, 0), (1, 0), (0, 0)))[:, :-1]
    nxt = jnp.pad(xf, ((0, 0), (0, 1), (0, 0)))[:, 1:]
    return prev * w[0] + xf * w[1] + nxt * w[2]


def rope_tables(rows):
    row_pos = jnp.repeat(jnp.arange(rows, dtype=F32), GRID_W)
    col_pos = jnp.tile(jnp.arange(GRID_W, dtype=F32), rows)
    axis_dim = ATTN_HD // 2
    inv_freq = ROPE_THETA ** (-jnp.arange(0, axis_dim, 2, dtype=F32) / axis_dim)
    ang = jnp.concatenate([row_pos[:, None] * inv_freq, col_pos[:, None] * inv_freq], axis=-1)
    return jnp.cos(ang), jnp.sin(ang)


def rope_halves(x, cos, sin):
    half = ATTN_HD // 2
    x1, x2 = x[..., :half], x[..., half:]
    cs, sn = cos[None, :, None, :], sin[None, :, None, :]
    return jnp.concatenate([x1 * cs - x2 * sn, x1 * sn + x2 * cs], axis=-1)


def _to_chunks(t):
    B, L = t.shape[:2]
    t = t.reshape((B, L // DN_CHUNK, DN_CHUNK) + t.shape[2:])
    return jnp.moveaxis(t, 3, 1)


def gated_delta_chunked(q, k, v, g, beta, s0):
    B, L, H, _ = k.shape
    kc, vc, gc, bc = (_to_chunks(t) for t in (k, v, g, beta))
    gcum = jnp.cumsum(gc, axis=-1)
    incl = jnp.tril(jnp.ones((DN_CHUNK, DN_CHUNK), bool))
    decay = jnp.exp(jnp.where(incl, gcum[..., :, None] - gcum[..., None, :], -jnp.inf))
    strict = jnp.tril(jnp.ones((DN_CHUNK, DN_CHUNK), F32), -1)
    kb = kc * bc[..., None]
    lmat = jnp.einsum('bhnid,bhnjd->bhnij', kb, kc) * decay * strict
    eye = jnp.eye(DN_CHUNK, dtype=F32)
    rhs = jnp.concatenate([vc * bc[..., None], kb * jnp.exp(gcum)[..., None]], axis=-1)
    sol = lax.linalg.triangular_solve(lmat + eye, rhs, left_side=True, lower=True, unit_diagonal=True)
    u0, wk = sol[..., :DN_DV], sol[..., DN_DV:]
    kdec = kc * jnp.exp(gcum[..., -1:] - gcum)[..., None]
    chunk_decay = jnp.exp(gcum[..., -1])
    with_out = q is not None
    xs = [u0, wk, kdec, chunk_decay]
    if with_out:
        qc = _to_chunks(q)
        xs += [jnp.einsum('bhnid,bhnjd->bhnij', qc, kc) * decay, qc * jnp.exp(gcum)[..., None]]
    xs = [jnp.moveaxis(t, 2, 0) for t in xs]

    def step(s, inp):
        u = inp[0] - jnp.einsum('bhcd,bhde->bhce', inp[1], s)
        s_new = s * inp[3][..., None, None] + jnp.einsum('bhcd,bhce->bhde', inp[2], u)
        if with_out:
            o = jnp.einsum('bhcd,bhde->bhce', inp[5], s) + jnp.einsum('bhcj,bhje->bhce', inp[4], u)
            return s_new, o
        return s_new, None

    s_fin, o = lax.scan(step, s0, xs)
    if not with_out:
        return s_fin, None
    o = jnp.moveaxis(jnp.moveaxis(o, 0, 2), 1, 3).reshape(B, L, H, DN_DV)
    return s_fin, o


def _rev(t, d):
    return jnp.flip(t, axis=1) if d == 1 else t


def _deinterleave_perm(n_heads):
    base = np.concatenate([np.arange(0, ATTN_HD, 2), np.arange(1, ATTN_HD, 2)])
    return np.concatenate([h * ATTN_HD + base for h in range(n_heads)])


def prep_w_in(w_in):
    D = w_in.shape[0]
    o_kv = 0
    o_a = DN_QK_W + DN_V_W
    o_atk = o_a + 4 * DN_HEADS
    o_atv = o_atk + ATTN_KV_W
    o_rest = o_atv + ATTN_KV_W
    o_atq = o_rest + DN_QK_W + DN_V_W + 3 * SC_WIDTH
    o_gates = o_atq + ATTN_Q_W
    w_kv = w_in[:, o_kv:o_a]
    w_ab = w_in[:, o_a:o_atk]
    w_atk = w_in[:, o_atk:o_atv][:, _deinterleave_perm(ATTN_KV_HEADS)]
    w_atv = w_in[:, o_atv:o_rest]
    w_pre_q = w_in[:, o_rest:o_atq]
    w_atq = w_in[:, o_atq:o_gates][:, _deinterleave_perm(ATTN_HEADS)]
    w_gates = w_in[:, o_gates:]
    w_main = jnp.concatenate([w_kv, w_atk, w_atv, w_pre_q, w_atq, w_gates], axis=1).astype(BF16)
    w_ab = jnp.pad(w_ab, ((0, 0), (0, LANES - w_ab.shape[1]))).astype(BF16)
    return w_main, w_ab


KV_MAIN = DN_QK_W + DN_V_W + 2 * ATTN_KV_W


def kv_side(p_kv, p_ab, w_conv_kv, dn_a_log, dn_dt_bias, k_norm_p):
    B, L = p_kv.shape[:2]
    dn_kv, at_k, at_v = _split(p_kv, (DN_QK_W + DN_V_W, ATTN_KV_W, ATTN_KV_W))
    dn_kv = jax.nn.silu(dwconv3(dn_kv, w_conv_kv))
    dn_k = l2norm(dn_kv[..., :DN_QK_W].reshape(B, L, DN_HEADS, DN_DK))
    dn_v = dn_kv[..., DN_QK_W:].reshape(B, L, DN_HEADS, DN_DV)
    a = p_ab[..., :2 * DN_HEADS].astype(F32).reshape(B, L, 2, DN_HEADS)
    dn_g = -jnp.exp(dn_a_log.astype(F32)) * jax.nn.softplus(a + dn_dt_bias.astype(F32))
    dn_beta = jax.nn.sigmoid(p_ab[..., 2 * DN_HEADS:4 * DN_HEADS].astype(F32).reshape(B, L, 2, DN_HEADS))
    at_k = rmsnorm(at_k.reshape(B, L, ATTN_KV_HEADS, ATTN_HD), k_norm_p)
    return dn_k, dn_v, dn_g, dn_beta, at_k, at_v


REST_SIZES = (DN_QK_W, DN_V_W, SC_WIDTH, SC_WIDTH, SC_WIDTH, ATTN_Q_W, N_BRANCH * 1024)


def rest_side(rest, w_conv_q, q_norm_p):
    B, L = rest.shape[:2]
    D = rest.shape[-1] - (DN_QK_W + DN_V_W + 3 * SC_WIDTH + ATTN_Q_W)
    dn_q, dn_z, sc_u, sc_b, sc_c, at_q, gates = _split(
        rest, (DN_QK_W, DN_V_W, SC_WIDTH, SC_WIDTH, SC_WIDTH, ATTN_Q_W, D))
    dn_q = jax.nn.silu(dwconv3(dn_q, w_conv_q))
    dn_q = l2norm(dn_q.reshape(B, L, DN_HEADS, DN_DK)) * (DN_DK ** -0.5)
    at_q = rmsnorm(at_q.reshape(B, L, ATTN_HEADS, ATTN_HD), q_norm_p)
    return dn_q, dn_z, sc_u, sc_b, sc_c, at_q, gates


def merge_branches(o_dn, dn_z, sc_u, sc_b, sc_c, o_attn, gates, dn_norm, w_sc_conv,
                   w_proj_dn, w_proj_sc, w_proj_attn, w_out):
    B, L = dn_z.shape[:2]
    D = w_out.shape[0]
    z = dn_z.astype(F32).reshape(B, L, DN_HEADS, DN_DV)
    a_dn = (rmsnorm(o_dn, dn_norm) * jax.nn.silu(z)).reshape(B * L, DN_V_W)
    a_sc = (sc_b.astype(F32) * dwconv3(sc_c.astype(F32) * sc_u.astype(F32), w_sc_conv)).reshape(B * L, SC_WIDTH)
    y_dn = matmul(a_dn, w_proj_dn, F32)
    y_sc = matmul(a_sc, w_proj_sc, F32)
    y_at = matmul(o_attn.reshape(B * L, ATTN_Q_W), w_proj_attn, F32)
    g_dn, g_sc, g_at = jnp.split(jax.nn.sigmoid(gates.astype(F32)).reshape(B * L, N_BRANCH * D), N_BRANCH, axis=-1)
    comb = g_dn * y_dn + g_sc * y_sc + g_at * y_at
    return matmul(comb, w_out, F32).reshape(B, L, D)


def mix_sublayer(hl, hc, w_in, w_dn_conv_q, w_dn_conv_kv, dn_a_log, dn_dt_bias, dn_norm, w_sc_conv,
                 q_norm, k_norm, w_proj_dn, w_proj_sc, w_proj_attn, w_out, cos, sin, ctx_out):
    B, L, D = hl.shape
    Lc = hc.shape[1]
    w_main, w_ab = prep_w_in(w_in)
    base = np.concatenate([np.arange(0, ATTN_HD, 2), np.arange(1, ATTN_HD, 2)])
    q_norm_p, k_norm_p = q_norm[base], k_norm[base]

    hl2 = hl.reshape(B * L, D).astype(BF16)
    hc2 = hc.reshape(B * Lc, D).astype(BF16)
    p_l = matmul(hl2, w_main, BF16).reshape(B, L, -1)
    ab_l = matmul(hl2, w_ab, F32).reshape(B, L, -1)
    p_c = matmul(hc2, w_main if ctx_out else w_main[:, :KV_MAIN], BF16).reshape(B, Lc, -1)
    ab_c = matmul(hc2, w_ab, F32).reshape(B, Lc, -1)

    dn_kl, dn_vl, dn_gl, dn_bl, at_kl, at_vl = kv_side(p_l[..., :KV_MAIN], ab_l, w_dn_conv_kv, dn_a_log, dn_dt_bias, k_norm_p)
    dn_kc, dn_vc, dn_gc, dn_bc, at_kc, at_vc = kv_side(p_c[..., :KV_MAIN], ab_c, w_dn_conv_kv, dn_a_log, dn_dt_bias, k_norm_p)
    dn_ql, z_l, scu_l, scb_l, scc_l, at_ql, gates_l = rest_side(p_l[..., KV_MAIN:], w_dn_conv_q, q_norm_p)
    at_ql = rope_halves(at_ql, cos, sin)
    at_kl = rope_halves(at_kl, cos, sin)
    dn_qc = None
    if ctx_out:
        dn_qc, z_c, scu_c, scb_c, scc_c, at_qc, gates_c = rest_side(p_c[..., KV_MAIN:], w_dn_conv_q, q_norm_p)

    s_zero = jnp.zeros((B, DN_HEADS, DN_DK, DN_DV), F32)
    o_dn_l, o_dn_c = [], []
    for d in range(2):
        q_ctx_d = None if dn_qc is None else _rev(dn_qc, d)
        s_ctx, o_c = gated_delta_chunked(q_ctx_d, _rev(dn_kc, d), _rev(dn_vc, d),
                                         _rev(dn_gc[:, :, d], d), _rev(dn_bc[:, :, d], d), s_zero)
        _, o_l = gated_delta_chunked(_rev(dn_ql, d), _rev(dn_kl, d), _rev(dn_vl, d),
                                     _rev(dn_gl[:, :, d], d), _rev(dn_bl[:, :, d], d), s_ctx)
        o_dn_l.append(_rev(o_l, d))
        if ctx_out:
            o_dn_c.append(_rev(o_c, d))

    scale = ATTN_HD ** -0.5
    k_all = jnp.concatenate([at_kc.reshape(B, Lc, ATTN_KV_W), at_kl.reshape(B, L, ATTN_KV_W)], axis=1)
    v_all = jnp.concatenate([at_vc, at_vl], axis=1)
    o_at_l = attention((at_ql * scale).reshape(B, L, ATTN_Q_W), k_all, v_all)
    y_l = merge_branches(o_dn_l[0] + o_dn_l[1], z_l, scu_l, scb_l, scc_l, o_at_l, gates_l, dn_norm, w_sc_conv,
                         w_proj_dn, w_proj_sc, w_proj_attn, w_out)
    if not ctx_out:
        return y_l, None
    o_at_c = attention((at_qc * scale).reshape(B, Lc, ATTN_Q_W), at_kc.reshape(B, Lc, ATTN_KV_W), at_vc)
    y_c = merge_branches(o_dn_c[0] + o_dn_c[1], z_c, scu_c, scb_c, scc_c, o_at_c, gates_c, dn_norm, w_sc_conv,
                         w_proj_dn, w_proj_sc, w_proj_attn, w_out)
    return y_l, y_c


def route(h, w_router, b_router):
    T = h.shape[0]
    w_pad = jnp.pad(w_router, ((0, 0), (0, LANES - N_EXPERTS)))
    logits = matmul(h, w_pad, F32)[:, :N_EXPERTS]
    scores = jax.nn.sigmoid(logits)
    sel = scores + b_router.astype(F32)
    grp = sel.reshape(T, N_GROUPS, N_EXPERTS // N_GROUPS)
    grp_score = jnp.sum(lax.top_k(grp, 2)[0], axis=-1)
    _, gidx = lax.top_k(grp_score, TOPK_GROUPS)
    gmask = jnp.any(gidx[:, :, None] == jnp.arange(N_GROUPS)[None, None, :], axis=1)
    emask = jnp.repeat(gmask, N_EXPERTS // N_GROUPS, axis=1)
    _, idx = lax.top_k(jnp.where(emask, sel, -jnp.inf), TOP_K)
    w = jnp.take_along_axis(scores, idx, axis=1)
    w = w / (jnp.sum(w, axis=-1, keepdims=True) + 1e-20) * ROUTED_SCALE
    return idx, w


def moe_ffn(h, w_router, b_router, w_gate, w_up, w_down, ws_gate, ws_up, ws_down):
    T, D = h.shape
    hb = h.astype(BF16)
    idx, wts = route(hb, w_router, b_router)
    n_assign = T * TOP_K
    n_blocks = -(-n_assign // MOE_BLOCK) + N_EXPERTS
    n_slots = n_blocks * MOE_BLOCK
    flat_e = idx.reshape(-1)
    flat_t = jnp.repeat(jnp.arange(T, dtype=jnp.int32), TOP_K)
    order = jnp.argsort(flat_e)
    e_sorted = flat_e[order]
    counts = jnp.bincount(flat_e, length=N_EXPERTS)
    padded = (counts + MOE_BLOCK - 1) // MOE_BLOCK * MOE_BLOCK
    pad_end = jnp.cumsum(padded)
    pad_start = pad_end - padded
    start = jnp.cumsum(counts) - counts
    dest = (pad_start[e_sorted] + jnp.arange(n_assign) - start[e_sorted]).astype(jnp.int32)
    slot_tok = jnp.full((n_slots,), T, jnp.int32).at[dest].set(flat_t[order])
    block_e = jnp.minimum(jnp.searchsorted(pad_end, jnp.arange(n_blocks) * MOE_BLOCK, side='right'), N_EXPERTS - 1)
    h_pad = jnp.concatenate([hb, jnp.zeros((1, D), BF16)], axis=0)
    x_sorted = h_pad[slot_tok]
    y_sorted = expert_blocks(x_sorted, block_e, w_gate.astype(BF16), w_up.astype(BF16), w_down.astype(BF16))
    pos = jnp.zeros((n_assign,), jnp.int32).at[order].set(dest).reshape(T, TOP_K)
    routed = jnp.sum(y_sorted[pos].astype(F32) * wts[:, :, None], axis=1)
    shared = expert_blocks(hb, jnp.zeros((T // MOE_BLOCK,), jnp.int32), ws_gate.astype(BF16)[None],
                           ws_up.astype(BF16)[None], ws_down.astype(BF16)[None])
    return routed + shared.astype(F32)


def kernel(x, c, ctx, c_ctx, w_mod, b_mod, mix_pre, mix_post, ffn_pre, ffn_post, w_in, w_dn_conv_q, w_dn_conv_kv, dn_a_log, dn_dt_bias, dn_norm, w_sc_conv, q_norm, k_norm, w_proj_dn, w_proj_sc, w_proj_attn, w_out, w_router, b_router, w_exp_gate, w_exp_up, w_exp_down, w_sh_gate, w_sh_up, w_sh_down):
    B, L, D = x.shape
    Lc = ctx.shape[1]
    depth = w_in.shape[0]
    cos, sin = rope_tables(L // GRID_W)
    h_lat, h_ctx = x, ctx
    silu_all = jax.nn.silu(jnp.concatenate([c, c_ctx[None]], axis=0))
    n_pad = -(-(B + 1) // 8) * 8
    silu_all = jnp.pad(silu_all, ((0, n_pad - (B + 1)), (0, 0)))
    for layer in range(depth):
        ctx_out = layer < depth - 1
        mod = matmul(silu_all, w_mod[layer], F32) + b_mod[layer]
        sh1, sc1, g1, sh2, sc2, g2 = (m[:, None, :] for m in jnp.split(mod[:B], 6, axis=-1))
        mc = jnp.split(mod[B], 6)
        hl = rmsnorm(h_lat, mix_pre[layer]) * (1 + sc1) + sh1
        hc = rmsnorm(h_ctx, mix_pre[layer]) * (1 + mc[1]) + mc[0]
        y_l, y_c = mix_sublayer(hl, hc, w_in[layer], w_dn_conv_q[layer], w_dn_conv_kv[layer], dn_a_log[layer],
                                dn_dt_bias[layer], dn_norm[layer], w_sc_conv[layer], q_norm[layer], k_norm[layer],
                                w_proj_dn[layer], w_proj_sc[layer], w_proj_attn[layer], w_out[layer], cos, sin, ctx_out)
        h_lat = h_lat + g1 * rmsnorm(y_l, mix_post[layer])
        fl = rmsnorm(h_lat, ffn_pre[layer]) * (1 + sc2) + sh2
        moe_w = (w_router[layer], b_router[layer], w_exp_gate[layer], w_exp_up[layer], w_exp_down[layer],
                 w_sh_gate[layer], w_sh_up[layer], w_sh_down[layer])
        if ctx_out:
            h_ctx = h_ctx + mc[2] * rmsnorm(y_c, mix_post[layer])
            fc = rmsnorm(h_ctx, ffn_pre[layer]) * (1 + mc[4]) + mc[3]
            f = moe_ffn(jnp.concatenate([fl.reshape(B * L, D), fc.reshape(B * Lc, D)], axis=0), *moe_w)
            f_l = f[:B * L].reshape(B, L, D)
            h_ctx = h_ctx + mc[5] * rmsnorm(f[B * L:].reshape(B, Lc, D), ffn_post[layer])
        else:
            f_l = moe_ffn(fl.reshape(B * L, D), *moe_w).reshape(B, L, D)
        h_lat = h_lat + g2 * rmsnorm(f_l, ffn_post[layer])
    return h_lat
```

```python
import functools
import math

import jax
import jax.numpy as jnp
import numpy as np
from jax import lax
from jax.experimental import pallas as pl
from jax.experimental.pallas import tpu as pltpu

F32 = jnp.float32
BF16 = jnp.bfloat16

GRID_W = 64
NORM_EPS = 1e-6
DN_HEADS = 4
DN_DK = 128
DN_DV = 128
DN_CHUNK = 64
SC_WIDTH = 512
ATTN_HEADS = 8
ATTN_KV_HEADS = 2
ATTN_GROUP = ATTN_HEADS // ATTN_KV_HEADS
ATTN_HD = 128
ROPE_THETA = 10000.0
N_EXPERTS = 64
TOP_K = 8
N_GROUPS = 8
TOPK_GROUPS = 4
ROUTED_SCALE = 2.5
MOE_BLOCK = 256
N_BRANCH = 3

DN_QK_W = DN_HEADS * DN_DK
DN_V_W = DN_HEADS * DN_DV
ATTN_Q_W = ATTN_HEADS * ATTN_HD
ATTN_KV_W = ATTN_KV_HEADS * ATTN_HD
LANES = 128


def _mm_kernel(x_ref, w_ref, o_ref):
    o_ref[...] = jnp.dot(x_ref[...], w_ref[...], preferred_element_type=F32).astype(o_ref.dtype)


def _pick_tile(n, want):
    t = min(n, want)
    while n % t:
        t //= 2
    return t


def matmul(x, w, out_dtype, tm=512, tn=1024):
    M, K = x.shape
    N = w.shape[1]
    tm = _pick_tile(M, tm)
    tn = _pick_tile(N, tn)
    return pl.pallas_call(
        _mm_kernel,
        grid=(N // tn, M // tm),
        in_specs=[pl.BlockSpec((tm, K), lambda j, i: (i, 0)),
                  pl.BlockSpec((K, tn), lambda j, i: (0, j))],
        out_specs=pl.BlockSpec((tm, tn), lambda j, i: (i, j)),
        out_shape=jax.ShapeDtypeStruct((M, N), out_dtype),
        name="matmul",
    )(x.astype(BF16), w.astype(BF16))


def _attn_kernel(q_ref, k_ref, v_ref, o_ref):
    q = q_ref[0]
    s = lax.dot_general(q, k_ref[0], (((1,), (1,)), ((), ())), preferred_element_type=F32)
    m = jnp.max(s, axis=-1, keepdims=True)
    p = jnp.exp(s - m)
    l = jnp.sum(p, axis=-1, keepdims=True)
    o = jnp.dot(p.astype(BF16), v_ref[0], preferred_element_type=F32)
    o_ref[0] = (o / l).astype(o_ref.dtype)


def attention(q, k, v, tq=512):
    B, Lq, _ = q.shape
    Lk = k.shape[1]
    tq = _pick_tile(Lq, tq)
    return pl.pallas_call(
        _attn_kernel,
        grid=(B, ATTN_KV_HEADS, ATTN_GROUP, Lq // tq),
        in_specs=[pl.BlockSpec((1, tq, ATTN_HD), lambda b, h, g, i: (b, i, h * ATTN_GROUP + g)),
                  pl.BlockSpec((1, Lk, ATTN_HD), lambda b, h, g, i: (b, 0, h)),
                  pl.BlockSpec((1, Lk, ATTN_HD), lambda b, h, g, i: (b, 0, h))],
        out_specs=pl.BlockSpec((1, tq, ATTN_HD), lambda b, h, g, i: (b, i, h * ATTN_GROUP + g)),
        out_shape=jax.ShapeDtypeStruct(q.shape, BF16),
        name="attention",
    )(q.astype(BF16), k.astype(BF16), v.astype(BF16))


def _expert_kernel(be_ref, x_ref, wg_ref, wu_ref, wd_ref, o_ref):
    del be_ref
    x = x_ref[...]
    g = jnp.dot(x, wg_ref[0], preferred_element_type=F32)
    u = jnp.dot(x, wu_ref[0], preferred_element_type=F32)
    a = (g * jax.nn.sigmoid(g) * u).astype(BF16)
    o_ref[...] = jnp.dot(a, wd_ref[0], preferred_element_type=F32).astype(o_ref.dtype)


def expert_blocks(x_sorted, block_e, wg, wu, wd, tm=MOE_BLOCK):
    n_rows, D = x_sorted.shape
    F = wg.shape[-1]
    n_blocks = n_rows // tm
    grid_spec = pltpu.PrefetchScalarGridSpec(
        num_scalar_prefetch=1,
        grid=(n_blocks,),
        in_specs=[pl.BlockSpec((tm, D), lambda i, be: (i, 0)),
                  pl.BlockSpec((1, D, F), lambda i, be: (be[i], 0, 0)),
                  pl.BlockSpec((1, D, F), lambda i, be: (be[i], 0, 0)),
                  pl.BlockSpec((1, F, D), lambda i, be: (be[i], 0, 0))],
        out_specs=pl.BlockSpec((tm, D), lambda i, be: (i, 0)),
    )
    return pl.pallas_call(
        _expert_kernel,
        grid_spec=grid_spec,
        out_shape=jax.ShapeDtypeStruct((n_rows, D), BF16),
        name="expert_blocks",
    )(block_e.astype(jnp.int32), x_sorted, wg, wu, wd)


DN_STACK = DN_HEADS * DN_CHUNK


def _delta_kernel(k_ref, v_ref, q_ref, gcol_ref, glast_ref, beta_ref, grow_ref, of_ref, ob_ref, s_ref,
                  *, n_ctx_chunks, n_chunks):
    C, R, H = DN_CHUNK, DN_STACK, DN_HEADS
    s_ref[...] = jnp.zeros_like(s_ref)
    row = lax.broadcasted_iota(jnp.int32, (R, R), 0)
    col = lax.broadcasted_iota(jnp.int32, (R, R), 1)
    same_head = (row // C) == (col // C)
    strict = (same_head & (col < row), same_head & (col > row))
    eye = (row == col).astype(F32)

    def stack_heads(x):
        return jnp.concatenate([x[:, h * DN_DK:(h + 1) * DN_DK] for h in range(H)], axis=0)

    def body(n, carry):
        for d in range(2):
            if d == 0:
                c = n
            else:
                c = jnp.where(n < n_ctx_chunks, n_ctx_chunks - 1 - n, n_chunks - 1 - (n - n_ctx_chunks))
            r0 = pl.multiple_of(c * C, C)
            ks = stack_heads(k_ref[0, pl.ds(r0, C), :]).astype(F32)
            vs = stack_heads(v_ref[0, pl.ds(r0, C), :]).astype(F32)
            qs = stack_heads(q_ref[0, pl.ds(r0, C), :]).astype(F32)

            def col_stack(ref):
                blk = ref[0, pl.ds(r0, C), :]
                return jnp.concatenate([blk[:, d * H + h:d * H + h + 1] for h in range(H)], axis=0)

            gc, gl, bt = col_stack(gcol_ref), col_stack(glast_ref), col_stack(beta_ref)
            gr = grow_ref[0, d, pl.ds(c, 1), :]
            ks_b = ks.astype(BF16)
            kb = ks * bt
            decay = jnp.exp(jnp.where(strict[d], gc - gr, -jnp.inf))
            g_kk = lax.dot_general(kb.astype(BF16), ks_b, (((1,), (1,)), ((), ())), preferred_element_type=F32)
            g_qk = lax.dot_general(qs.astype(BF16), ks_b, (((1,), (1,)), ((), ())), preferred_element_type=F32)
            p = (-(g_kk * decay)).astype(BF16)
            qk = (g_qk * (decay + eye)).astype(BF16)
            e_gc = jnp.exp(gc)
            x = jnp.concatenate([vs * bt, kb * e_gc], axis=1)
            for it in range(6):
                x = x + jnp.dot(p, x.astype(BF16), preferred_element_type=F32)
                if it < 5:
                    p = jnp.dot(p, p, preferred_element_type=F32).astype(BF16)
            u0, wk = x[:, :DN_DV], x[:, DN_DV:]
            kdec = (ks * jnp.exp(gl - gc)).astype(BF16)
            qg = (qs * e_gc).astype(BF16)
            wk_b = wk.astype(BF16)
            us, os_ = [], []
            for h in range(H):
                sl = slice(h * C, (h + 1) * C)
                s_old = s_ref[d, h]
                s_b = s_old.astype(BF16)
                u_h = u0[sl] - jnp.dot(wk_b[sl], s_b, preferred_element_type=F32)
                os_.append(jnp.dot(qg[sl], s_b, preferred_element_type=F32))
                cd = jnp.exp(gl[h * C:h * C + 1, :])
                s_ref[d, h] = s_old * cd + lax.dot_general(kdec[sl], u_h.astype(BF16), (((0,), (0,)), ((), ())),
                                                           preferred_element_type=F32)
                us.append(u_h)
            o = jnp.concatenate(os_, axis=0) + jnp.dot(qk, jnp.concatenate(us, axis=0).astype(BF16),
                                                       preferred_element_type=F32)
            o_ref = of_ref if d == 0 else ob_ref
            for h in range(H):
                o_ref[0, pl.ds(r0, C), h * DN_DV:(h + 1) * DN_DV] = o[h * C:(h + 1) * C].astype(o_ref.dtype)
        return carry

    lax.fori_loop(0, n_chunks, body, 0)


def delta_rule(k, v, q, g, beta, n_ctx):
    B, Lt, W = k.shape
    H, C = DN_HEADS, DN_CHUNK
    nc = Lt // C
    gch = g.reshape(B, nc, C, 2, H)
    g_f = jnp.cumsum(gch[:, :, :, 0], axis=2)
    g_b = jnp.flip(jnp.cumsum(jnp.flip(gch[:, :, :, 1], axis=2), axis=2), axis=2)
    gcum = jnp.stack([g_f, g_b], axis=3)
    gtot = jnp.broadcast_to(jnp.stack([g_f[:, :, -1:], g_b[:, :, :1]], axis=3), gcum.shape)
    gcol = gcum.reshape(B, Lt, 2 * H)
    glast = gtot.reshape(B, Lt, 2 * H)
    grow = jnp.transpose(gcum, (0, 3, 1, 4, 2)).reshape(B, 2, nc, H * C)
    bt = beta.reshape(B, Lt, 2 * H)
    seq = lambda w: pl.BlockSpec((1, Lt, w), lambda b: (b, 0, 0))
    kern = functools.partial(_delta_kernel, n_ctx_chunks=n_ctx // C, n_chunks=nc)
    return pl.pallas_call(
        kern,
        grid=(B,),
        in_specs=[seq(W), seq(W), seq(W), seq(2 * H), seq(2 * H), seq(2 * H),
                  pl.BlockSpec((1, 2, nc, H * C), lambda b: (b, 0, 0, 0))],
        out_specs=[seq(W), seq(W)],
        out_shape=[jax.ShapeDtypeStruct((B, Lt, W), BF16)] * 2,
        scratch_shapes=[pltpu.VMEM((2, H, DN_DK, DN_DV), F32)],
        compiler_params=pltpu.CompilerParams(vmem_limit_bytes=48 * 1024 * 1024),
        name="delta_rule",
    )(k, v, q, gcol, glast, bt, grow)


ROUTE_TILE = 512


def _route_kernel(h_ref, w_ref, b_ref, idx_ref, rank_ref, wt_ref, cnt_ref, carry_ref):
    E, G, tn = N_EXPERTS, N_GROUPS, h_ref.shape[0]
    per = E // G
    neg = -jnp.inf

    @pl.when(pl.program_id(0) == 0)
    def _():
        carry_ref[...] = jnp.zeros_like(carry_ref)

    logits = lax.dot_general(w_ref[...], h_ref[...], (((1,), (1,)), ((), ())), preferred_element_type=F32)
    scores = jax.nn.sigmoid(logits)
    sel = scores + b_ref[...]
    iota_p = lax.broadcasted_iota(jnp.int32, (per, tn), 0)
    gs = []
    for g in range(G):
        blk = sel[g * per:(g + 1) * per]
        m1 = jnp.max(blk, axis=0, keepdims=True)
        i1 = jnp.min(jnp.where(blk == m1, iota_p, per), axis=0, keepdims=True)
        m2 = jnp.max(jnp.where(iota_p == i1, neg, blk), axis=0, keepdims=True)
        gs.append(m1 + m2)
    gsel = jnp.concatenate(gs, axis=0)
    iota_g = lax.broadcasted_iota(jnp.int32, (G, tn), 0)
    gpick = jnp.zeros((G, tn), F32)
    for _ in range(TOPK_GROUPS):
        m = jnp.max(gsel, axis=0, keepdims=True)
        i = jnp.min(jnp.where(gsel == m, iota_g, G), axis=0, keepdims=True)
        hit = iota_g == i
        gpick = jnp.where(hit, 1.0, gpick)
        gsel = jnp.where(hit, neg, gsel)
    emask = jnp.concatenate([jnp.broadcast_to(gpick[g:g + 1], (per, tn)) for g in range(G)], axis=0) > 0.5
    cand = jnp.where(emask, sel, neg)
    iota_e = lax.broadcasted_iota(jnp.int32, (E, tn), 0)
    picked = jnp.zeros((E, tn), F32)
    ids, pick_scores = [], []
    for _ in range(TOP_K):
        m = jnp.max(cand, axis=0, keepdims=True)
        i = jnp.min(jnp.where(cand == m, iota_e, E), axis=0, keepdims=True)
        hit = iota_e == i
        ids.append(i)
        pick_scores.append(jnp.sum(jnp.where(hit, scores, 0.0), axis=0, keepdims=True))
        picked = jnp.where(hit, 1.0, picked)
        cand = jnp.where(hit, neg, cand)
    total = pick_scores[0]
    for sc in pick_scores[1:]:
        total = total + sc
    rr = lax.broadcasted_iota(jnp.int32, (tn, tn), 0)
    cc = lax.broadcasted_iota(jnp.int32, (tn, tn), 1)
    before = (rr < cc).astype(BF16)
    rank = jnp.dot(picked.astype(BF16), before, preferred_element_type=F32) + carry_ref[:, :1]
    ranks = [jnp.sum(jnp.where(iota_e == i, rank, 0.0), axis=0, keepdims=True) for i in ids]
    idx_ref[...] = jnp.concatenate(ids, axis=0)
    rank_ref[...] = jnp.concatenate(ranks, axis=0).astype(jnp.int32)
    wt_ref[...] = jnp.concatenate([sc / (total + 1e-20) * ROUTED_SCALE for sc in pick_scores], axis=0)
    carry_ref[...] = carry_ref[...] + jnp.sum(picked, axis=1, keepdims=True)
    cnt_ref[...] = carry_ref[...]


def route(h, w_router, b_router):
    T, D = h.shape
    tn = _pick_tile(T, ROUTE_TILE)
    pick = pl.BlockSpec((TOP_K, tn), lambda i: (0, i))
    idx, rank, wts, cnt = pl.pallas_call(
        _route_kernel,
        grid=(T // tn,),
        in_specs=[pl.BlockSpec((tn, D), lambda i: (i, 0)),
                  pl.BlockSpec((N_EXPERTS, D), lambda i: (0, 0)),
                  pl.BlockSpec((N_EXPERTS, 1), lambda i: (0, 0))],
        out_specs=[pick, pick, pick, pl.BlockSpec((N_EXPERTS, LANES), lambda i: (0, 0))],
        out_shape=[jax.ShapeDtypeStruct((TOP_K, T), jnp.int32), jax.ShapeDtypeStruct((TOP_K, T), jnp.int32),
                   jax.ShapeDtypeStruct((TOP_K, T), F32), jax.ShapeDtypeStruct((N_EXPERTS, LANES), F32)],
        scratch_shapes=[pltpu.VMEM((N_EXPERTS, LANES), F32)],
        compiler_params=pltpu.CompilerParams(dimension_semantics=("arbitrary",)),
        name="route",
    )(h, w_router.T.astype(BF16), b_router.astype(F32)[:, None])
    return idx, rank, wts, cnt[:, 0].astype(jnp.int32)


def _split(x, sizes):
    cuts = [int(v) for v in np.cumsum(sizes)[:-1]]
    return jnp.split(x, cuts, axis=-1)


def rmsnorm(x, w):
    xf = x.astype(F32)
    y = xf * lax.rsqrt(jnp.mean(xf * xf, axis=-1, keepdims=True) + NORM_EPS)
    return y * w.astype(F32)


def l2norm(x):
    return x * lax.rsqrt(jnp.sum(x * x, axis=-1, keepdims=True) + NORM_EPS)


def dwconv3(x, w):
    xf = x.astype(F32)
    prev = jnp.pad(xf, ((0, 0), (1, 0), (0, 0)))[:, :-1]
    nxt = jnp.pad(xf, ((0, 0), (0, 1), (0, 0)))[:, 1:]
    return prev * w[0] + xf * w[1] + nxt * w[2]


def rope_tables(rows):
    row_pos = jnp.repeat(jnp.arange(rows, dtype=F32), GRID_W)
    col_pos = jnp.tile(jnp.arange(GRID_W, dtype=F32), rows)
    axis_dim = ATTN_HD // 2
    inv_freq = ROPE_THETA ** (-jnp.arange(0, axis_dim, 2, dtype=F32) / axis_dim)
    ang = jnp.concatenate([row_pos[:, None] * inv_freq, col_pos[:, None] * inv_freq], axis=-1)
    return jnp.cos(ang), jnp.sin(ang)


def rope_halves(x, cos, sin):
    half = ATTN_HD // 2
    x1, x2 = x[..., :half], x[..., half:]
    cs, sn = cos[None, :, None, :], sin[None, :, None, :]
    return jnp.concatenate([x1 * cs - x2 * sn, x1 * sn + x2 * cs], axis=-1)


def _deinterleave_perm(n_heads):
    base = np.concatenate([np.arange(0, ATTN_HD, 2), np.arange(1, ATTN_HD, 2)])
    return np.concatenate([h * ATTN_HD + base for h in range(n_heads)])


def prep_w_in(w_in):
    D = w_in.shape[0]
    o_kv = 0
    o_a = DN_QK_W + DN_V_W
    o_atk = o_a + 4 * DN_HEADS
    o_atv = o_atk + ATTN_KV_W
    o_rest = o_atv + ATTN_KV_W
    o_atq = o_rest + DN_QK_W + DN_V_W + 3 * SC_WIDTH
    o_gates = o_atq + ATTN_Q_W
    w_kv = w_in[:, o_kv:o_a]
    w_ab = w_in[:, o_a:o_atk]
    w_atk = w_in[:, o_atk:o_atv][:, _deinterleave_perm(ATTN_KV_HEADS)]
    w_atv = w_in[:, o_atv:o_rest]
    w_pre_q = w_in[:, o_rest:o_atq]
    w_atq = w_in[:, o_atq:o_gates][:, _deinterleave_perm(ATTN_HEADS)]
    w_gates = w_in[:, o_gates:]
    w_main = jnp.concatenate([w_kv, w_atk, w_atv, w_pre_q, w_atq, w_gates], axis=1).astype(BF16)
    w_ab = jnp.pad(w_ab, ((0, 0), (0, LANES - w_ab.shape[1]))).astype(BF16)
    return w_main, w_ab


KV_MAIN = DN_QK_W + DN_V_W + 2 * ATTN_KV_W


def kv_side(p_kv, p_ab, w_conv_kv, dn_a_log, dn_dt_bias, k_norm_p):
    B, L = p_kv.shape[:2]
    dn_kv, at_k, at_v = _split(p_kv, (DN_QK_W + DN_V_W, ATTN_KV_W, ATTN_KV_W))
    dn_kv = jax.nn.silu(dwconv3(dn_kv, w_conv_kv))
    dn_k = l2norm(dn_kv[..., :DN_QK_W].reshape(B, L, DN_HEADS, DN_DK))
    dn_v = dn_kv[..., DN_QK_W:].reshape(B, L, DN_HEADS, DN_DV)
    a = p_ab[..., :2 * DN_HEADS].astype(F32).reshape(B, L, 2, DN_HEADS)
    dn_g = -jnp.exp(dn_a_log.astype(F32)) * jax.nn.softplus(a + dn_dt_bias.astype(F32))
    dn_beta = jax.nn.sigmoid(p_ab[..., 2 * DN_HEADS:4 * DN_HEADS].astype(F32).reshape(B, L, 2, DN_HEADS))
    at_k = rmsnorm(at_k.reshape(B, L, ATTN_KV_HEADS, ATTN_HD), k_norm_p)
    return dn_k, dn_v, dn_g, dn_beta, at_k, at_v


REST_SIZES = (DN_QK_W, DN_V_W, SC_WIDTH, SC_WIDTH, SC_WIDTH, ATTN_Q_W, N_BRANCH * 1024)


def rest_side(rest, w_conv_q, q_norm_p):
    B, L = rest.shape[:2]
    D = rest.shape[-1] - (DN_QK_W + DN_V_W + 3 * SC_WIDTH + ATTN_Q_W)
    dn_q, dn_z, sc_u, sc_b, sc_c, at_q, gates = _split(
        rest, (DN_QK_W, DN_V_W, SC_WIDTH, SC_WIDTH, SC_WIDTH, ATTN_Q_W, D))
    dn_q = jax.nn.silu(dwconv3(dn_q, w_conv_q))
    dn_q = l2norm(dn_q.reshape(B, L, DN_HEADS, DN_DK)) * (DN_DK ** -0.5)
    at_q = rmsnorm(at_q.reshape(B, L, ATTN_HEADS, ATTN_HD), q_norm_p)
    return dn_q, dn_z, sc_u, sc_b, sc_c, at_q, gates


def merge_branches(o_dn, dn_z, sc_u, sc_b, sc_c, o_attn, gates, dn_norm, w_sc_conv,
                   w_proj_dn, w_proj_sc, w_proj_attn, w_out):
    B, L = dn_z.shape[:2]
    D = w_out.shape[0]
    z = dn_z.astype(F32).reshape(B, L, DN_HEADS, DN_DV)
    a_dn = (rmsnorm(o_dn, dn_norm) * jax.nn.silu(z)).reshape(B * L, DN_V_W)
    a_sc = (sc_b.astype(F32) * dwconv3(sc_c.astype(F32) * sc_u.astype(F32), w_sc_conv)).reshape(B * L, SC_WIDTH)
    y_dn = matmul(a_dn, w_proj_dn, F32)
    y_sc = matmul(a_sc, w_proj_sc, F32)
    y_at = matmul(o_attn.reshape(B * L, ATTN_Q_W), w_proj_attn, F32)
    g_dn, g_sc, g_at = jnp.split(jax.nn.sigmoid(gates.astype(F32)).reshape(B * L, N_BRANCH * D), N_BRANCH, axis=-1)
    comb = g_dn * y_dn + g_sc * y_sc + g_at * y_at
    return matmul(comb, w_out, F32).reshape(B, L, D)


def mix_sublayer(hl, hc, w_in, w_dn_conv_q, w_dn_conv_kv, dn_a_log, dn_dt_bias, dn_norm, w_sc_conv,
                 q_norm, k_norm, w_proj_dn, w_proj_sc, w_proj_attn, w_out, cos, sin, ctx_out):
    B, L, D = hl.shape
    Lc = hc.shape[1]
    w_main, w_ab = prep_w_in(w_in)
    base = np.concatenate([np.arange(0, ATTN_HD, 2), np.arange(1, ATTN_HD, 2)])
    q_norm_p, k_norm_p = q_norm[base], k_norm[base]

    hl2 = hl.reshape(B * L, D).astype(BF16)
    hc2 = hc.reshape(B * Lc, D).astype(BF16)
    p_l = matmul(hl2, w_main, BF16).reshape(B, L, -1)
    ab_l = matmul(hl2, w_ab, F32).reshape(B, L, -1)
    p_c = matmul(hc2, w_main if ctx_out else w_main[:, :KV_MAIN], BF16).reshape(B, Lc, -1)
    ab_c = matmul(hc2, w_ab, F32).reshape(B, Lc, -1)

    dn_kl, dn_vl, dn_gl, dn_bl, at_kl, at_vl = kv_side(p_l[..., :KV_MAIN], ab_l, w_dn_conv_kv, dn_a_log, dn_dt_bias, k_norm_p)
    dn_kc, dn_vc, dn_gc, dn_bc, at_kc, at_vc = kv_side(p_c[..., :KV_MAIN], ab_c, w_dn_conv_kv, dn_a_log, dn_dt_bias, k_norm_p)
    dn_ql, z_l, scu_l, scb_l, scc_l, at_ql, gates_l = rest_side(p_l[..., KV_MAIN:], w_dn_conv_q, q_norm_p)
    at_ql = rope_halves(at_ql, cos, sin)
    at_kl = rope_halves(at_kl, cos, sin)
    dn_qc = None
    if ctx_out:
        dn_qc, z_c, scu_c, scb_c, scc_c, at_qc, gates_c = rest_side(p_c[..., KV_MAIN:], w_dn_conv_q, q_norm_p)

    if dn_qc is None:
        dn_qc = jnp.zeros((B, Lc, DN_HEADS, DN_DK), F32)
    cat = lambda c_part, l_part: jnp.concatenate(
        [c_part.reshape(B, Lc, -1), l_part.reshape(B, L, -1)], axis=1).astype(BF16)
    o_f, o_b = delta_rule(cat(dn_kc, dn_kl), cat(dn_vc, dn_vl), cat(dn_qc, dn_ql),
                          jnp.concatenate([dn_gc, dn_gl], axis=1), jnp.concatenate([dn_bc, dn_bl], axis=1), Lc)
    o_dn = (o_f.astype(F32) + o_b.astype(F32)).reshape(B, Lc + L, DN_HEADS, DN_DV)
    o_dn_c, o_dn_l = o_dn[:, :Lc], o_dn[:, Lc:]

    scale = ATTN_HD ** -0.5
    k_all = jnp.concatenate([at_kc.reshape(B, Lc, ATTN_KV_W), at_kl.reshape(B, L, ATTN_KV_W)], axis=1)
    v_all = jnp.concatenate([at_vc, at_vl], axis=1)
    o_at_l = attention((at_ql * scale).reshape(B, L, ATTN_Q_W), k_all, v_all)
    y_l = merge_branches(o_dn_l, z_l, scu_l, scb_l, scc_l, o_at_l, gates_l, dn_norm, w_sc_conv,
                         w_proj_dn, w_proj_sc, w_proj_attn, w_out)
    if not ctx_out:
        return y_l, None
    o_at_c = attention((at_qc * scale).reshape(B, Lc, ATTN_Q_W), at_kc.reshape(B, Lc, ATTN_KV_W), at_vc)
    y_c = merge_branches(o_dn_c, z_c, scu_c, scb_c, scc_c, o_at_c, gates_c, dn_norm, w_sc_conv,
                         w_proj_dn, w_proj_sc, w_proj_attn, w_out)
    return y_l, y_c


def moe_ffn(h, w_router, b_router, w_gate, w_up, w_down, ws_gate, ws_up, ws_down):
    T, D = h.shape
    hb = h.astype(BF16)
    idx, rank, wts, counts = route(hb, w_router, b_router)
    n_assign = T * TOP_K
    n_blocks = -(-n_assign // MOE_BLOCK) + N_EXPERTS
    n_slots = n_blocks * MOE_BLOCK
    padded = (counts + MOE_BLOCK - 1) // MOE_BLOCK * MOE_BLOCK
    pad_end = jnp.cumsum(padded)
    pad_start = pad_end - padded
    dest = pad_start[idx] + rank
    tok = jnp.broadcast_to(jnp.arange(T, dtype=jnp.int32)[None], (TOP_K, T))
    slot_tok = jnp.full((n_slots,), T, jnp.int32).at[dest.reshape(-1)].set(tok.reshape(-1))
    block_e = jnp.minimum(jnp.searchsorted(pad_end, jnp.arange(n_blocks) * MOE_BLOCK, side='right'), N_EXPERTS - 1)
    h_pad = jnp.concatenate([hb, jnp.zeros((1, D), BF16)], axis=0)
    x_sorted = h_pad[slot_tok]
    y_sorted = expert_blocks(x_sorted, block_e, w_gate.astype(BF16), w_up.astype(BF16), w_down.astype(BF16))
    routed = jnp.sum(y_sorted[dest].astype(F32) * wts[:, :, None], axis=0)
    shared = expert_blocks(hb, jnp.zeros((T // MOE_BLOCK,), jnp.int32), ws_gate.astype(BF16)[None],
                           ws_up.astype(BF16)[None], ws_down.astype(BF16)[None])
    return routed + shared.astype(F32)


def kernel(x, c, ctx, c_ctx, w_mod, b_mod, mix_pre, mix_post, ffn_pre, ffn_post, w_in, w_dn_conv_q, w_dn_conv_kv, dn_a_log, dn_dt_bias, dn_norm, w_sc_conv, q_norm, k_norm, w_proj_dn, w_proj_sc, w_proj_attn, w_out, w_router, b_router, w_exp_gate, w_exp_up, w_exp_down, w_sh_gate, w_sh_up, w_sh_down):
    B, L, D = x.shape
    Lc = ctx.shape[1]
    depth = w_in.shape[0]
    cos, sin = rope_tables(L // GRID_W)
    h_lat, h_ctx = x, ctx
    silu_all = jax.nn.silu(jnp.concatenate([c, c_ctx[None]], axis=0))
    n_pad = -(-(B + 1) // 8) * 8
    silu_all = jnp.pad(silu_all, ((0, n_pad - (B + 1)), (0, 0)))
    for layer in range(depth):
        ctx_out = layer < depth - 1
        mod = matmul(silu_all, w_mod[layer], F32) + b_mod[layer]
        sh1, sc1, g1, sh2, sc2, g2 = (m[:, None, :] for m in jnp.split(mod[:B], 6, axis=-1))
        mc = jnp.split(mod[B], 6)
        hl = rmsnorm(h_lat, mix_pre[layer]) * (1 + sc1) + sh1
        hc = rmsnorm(h_ctx, mix_pre[layer]) * (1 + mc[1]) + mc[0]
        y_l, y_c = mix_sublayer(hl, hc, w_in[layer], w_dn_conv_q[layer], w_dn_conv_kv[layer], dn_a_log[layer],
                                dn_dt_bias[layer], dn_norm[layer], w_sc_conv[layer], q_norm[layer], k_norm[layer],
                                w_proj_dn[layer], w_proj_sc[layer], w_proj_attn[layer], w_out[layer], cos, sin, ctx_out)
        h_lat = h_lat + g1 * rmsnorm(y_l, mix_post[layer])
        fl = rmsnorm(h_lat, ffn_pre[layer]) * (1 + sc2) + sh2
        moe_w = (w_router[layer], b_router[layer], w_exp_gate[layer], w_exp_up[layer], w_exp_down[layer],
                 w_sh_gate[layer], w_sh_up[layer], w_sh_down[layer])
        if ctx_out:
            h_ctx = h_ctx + mc[2] * rmsnorm(y_c, mix_post[layer])
            fc = rmsnorm(h_ctx, ffn_pre[layer]) * (1 + mc[4]) + mc[3]
            f = moe_ffn(jnp.concatenate([fl.reshape(B * L, D), fc.reshape(B * Lc, D)], axis=0), *moe_w)
            f_l = f[:B * L].reshape(B, L, D)
            h_ctx = h_ctx + mc[5] * rmsnorm(f[B * L:].reshape(B, Lc, D), ffn_post[layer])
        else:
            f_l = moe_ffn(fl.reshape(B * L, D), *moe_w).reshape(B, L, D)
        h_lat = h_lat + g2 * rmsnorm(f_l, ffn_post[layer])
    return h_lat
```

```python
import functools
import math

import jax
import jax.numpy as jnp
import numpy as np
from jax import lax
from jax.experimental import pallas as pl
from jax.experimental.pallas import tpu as pltpu

F32 = jnp.float32
BF16 = jnp.bfloat16

GRID_W = 64
NORM_EPS = 1e-6
DN_HEADS = 4
DN_DK = 128
DN_DV = 128
DN_CHUNK = 64
SC_WIDTH = 512
ATTN_HEADS = 8
ATTN_KV_HEADS = 2
ATTN_GROUP = ATTN_HEADS // ATTN_KV_HEADS
ATTN_HD = 128
ROPE_THETA = 10000.0
N_EXPERTS = 64
TOP_K = 8
N_GROUPS = 8
TOPK_GROUPS = 4
ROUTED_SCALE = 2.5
MOE_BLOCK = 256
N_BRANCH = 3

DN_QK_W = DN_HEADS * DN_DK
DN_V_W = DN_HEADS * DN_DV
ATTN_Q_W = ATTN_HEADS * ATTN_HD
ATTN_KV_W = ATTN_KV_HEADS * ATTN_HD
LANES = 128


def _mm_kernel(x_ref, w_ref, o_ref):
    o_ref[...] = jnp.dot(x_ref[...], w_ref[...], preferred_element_type=F32).astype(o_ref.dtype)


def _pick_tile(n, want):
    t = min(n, want)
    while n % t:
        t //= 2
    return t


def matmul(x, w, out_dtype, tm=512, tn=1024):
    M, K = x.shape
    N = w.shape[1]
    tm = _pick_tile(M, tm)
    tn = _pick_tile(N, tn)
    return pl.pallas_call(
        _mm_kernel,
        grid=(N // tn, M // tm),
        in_specs=[pl.BlockSpec((tm, K), lambda j, i: (i, 0)),
                  pl.BlockSpec((K, tn), lambda j, i: (0, j))],
        out_specs=pl.BlockSpec((tm, tn), lambda j, i: (i, j)),
        out_shape=jax.ShapeDtypeStruct((M, N), out_dtype),
        name="matmul",
    )(x.astype(BF16), w.astype(BF16))


def _attn_kernel(q_ref, k_ref, v_ref, o_ref):
    k, v = k_ref[0], v_ref[0]
    for g in range(ATTN_GROUP):
        cols = slice(g * ATTN_HD, (g + 1) * ATTN_HD)
        s = lax.dot_general(q_ref[0, :, cols], k, (((1,), (1,)), ((), ())), preferred_element_type=F32)
        m = jnp.max(s, axis=-1, keepdims=True)
        p = jnp.exp(s - m)
        l = jnp.sum(p, axis=-1, keepdims=True)
        o = jnp.dot(p.astype(BF16), v, preferred_element_type=F32)
        o_ref[0, :, cols] = (o / l).astype(o_ref.dtype)


def attention(q, k, v, tq=256):
    B, Lq, _ = q.shape
    Lk = k.shape[1]
    tq = _pick_tile(Lq, tq)
    gw = ATTN_GROUP * ATTN_HD
    return pl.pallas_call(
        _attn_kernel,
        grid=(B, ATTN_KV_HEADS, Lq // tq),
        in_specs=[pl.BlockSpec((1, tq, gw), lambda b, h, i: (b, i, h)),
                  pl.BlockSpec((1, Lk, ATTN_HD), lambda b, h, i: (b, 0, h)),
                  pl.BlockSpec((1, Lk, ATTN_HD), lambda b, h, i: (b, 0, h))],
        out_specs=pl.BlockSpec((1, tq, gw), lambda b, h, i: (b, i, h)),
        out_shape=jax.ShapeDtypeStruct(q.shape, BF16),
        name="attention",
    )(q.astype(BF16), k.astype(BF16), v.astype(BF16))


def _expert_kernel(be_ref, x_ref, wg_ref, wu_ref, wd_ref, o_ref):
    del be_ref
    x = x_ref[...]
    g = jnp.dot(x, wg_ref[0], preferred_element_type=F32)
    u = jnp.dot(x, wu_ref[0], preferred_element_type=F32)
    a = (g * jax.nn.sigmoid(g) * u).astype(BF16)
    o_ref[...] = jnp.dot(a, wd_ref[0], preferred_element_type=F32).astype(o_ref.dtype)


def expert_blocks(x_sorted, block_e, wg, wu, wd, tm=MOE_BLOCK):
    n_rows, D = x_sorted.shape
    F = wg.shape[-1]
    n_blocks = n_rows // tm
    grid_spec = pltpu.PrefetchScalarGridSpec(
        num_scalar_prefetch=1,
        grid=(n_blocks,),
        in_specs=[pl.BlockSpec((tm, D), lambda i, be: (i, 0)),
                  pl.BlockSpec((1, D, F), lambda i, be: (be[i], 0, 0)),
                  pl.BlockSpec((1, D, F), lambda i, be: (be[i], 0, 0)),
                  pl.BlockSpec((1, F, D), lambda i, be: (be[i], 0, 0))],
        out_specs=pl.BlockSpec((tm, D), lambda i, be: (i, 0)),
    )
    return pl.pallas_call(
        _expert_kernel,
        grid_spec=grid_spec,
        out_shape=jax.ShapeDtypeStruct((n_rows, D), BF16),
        name="expert_blocks",
    )(block_e.astype(jnp.int32), x_sorted, wg, wu, wd)


DN_STACK = DN_HEADS * DN_CHUNK


def _delta_kernel(k_ref, v_ref, q_ref, gcol_ref, glast_ref, beta_ref, grow_ref, of_ref, ob_ref, s_ref,
                  *, n_ctx_chunks, n_chunks):
    C, R, H = DN_CHUNK, DN_STACK, DN_HEADS
    s_ref[...] = jnp.zeros_like(s_ref)
    row = lax.broadcasted_iota(jnp.int32, (R, R), 0)
    col = lax.broadcasted_iota(jnp.int32, (R, R), 1)
    same_head = (row // C) == (col // C)
    strict = (same_head & (col < row), same_head & (col > row))
    eye = (row == col).astype(F32)

    def stack_heads(x):
        return jnp.concatenate([x[:, h * DN_DK:(h + 1) * DN_DK] for h in range(H)], axis=0)

    def body(n, carry):
        for d in range(2):
            if d == 0:
                c = n
            else:
                c = jnp.where(n < n_ctx_chunks, n_ctx_chunks - 1 - n, n_chunks - 1 - (n - n_ctx_chunks))
            r0 = pl.multiple_of(c * C, C)
            ks = stack_heads(k_ref[0, pl.ds(r0, C), :]).astype(F32)
            vs = stack_heads(v_ref[0, pl.ds(r0, C), :]).astype(F32)
            qs = stack_heads(q_ref[0, pl.ds(r0, C), :]).astype(F32)

            def col_stack(ref):
                blk = ref[0, pl.ds(r0, C), :]
                return jnp.concatenate([blk[:, d * H + h:d * H + h + 1] for h in range(H)], axis=0)

            gc, gl, bt = col_stack(gcol_ref), col_stack(glast_ref), col_stack(beta_ref)
            gr = grow_ref[0, d, pl.ds(c, 1), :]
            ks_b = ks.astype(BF16)
            kb = ks * bt
            decay = jnp.exp(jnp.where(strict[d], gc - gr, -jnp.inf))
            g_kk = lax.dot_general(kb.astype(BF16), ks_b, (((1,), (1,)), ((), ())), preferred_element_type=F32)
            g_qk = lax.dot_general(qs.astype(BF16), ks_b, (((1,), (1,)), ((), ())), preferred_element_type=F32)
            p = (-(g_kk * decay)).astype(BF16)
            qk = (g_qk * (decay + eye)).astype(BF16)
            e_gc = jnp.exp(gc)
            x = jnp.concatenate([vs * bt, kb * e_gc], axis=1)
            for it in range(6):
                x_hi = x.astype(BF16)
                x_lo = (x - x_hi.astype(F32)).astype(BF16)
                x = x + (jnp.dot(p, x_hi, preferred_element_type=F32) + jnp.dot(p, x_lo, preferred_element_type=F32))
                if it < 5:
                    p = jnp.dot(p, p, preferred_element_type=F32).astype(BF16)
            u0, wk = x[:, :DN_DV], x[:, DN_DV:]
            kdec = (ks * jnp.exp(gl - gc)).astype(BF16)
            qg = (qs * e_gc).astype(BF16)
            wk_b = wk.astype(BF16)
            us, os_ = [], []
            for h in range(H):
                sl = slice(h * C, (h + 1) * C)
                s_old = s_ref[d, h]
                s_b = s_old.astype(BF16)
                u_h = u0[sl] - jnp.dot(wk_b[sl], s_b, preferred_element_type=F32)
                os_.append(jnp.dot(qg[sl], s_b, preferred_element_type=F32))
                cd = jnp.exp(gl[h * C:h * C + 1, :])
                s_ref[d, h] = s_old * cd + lax.dot_general(kdec[sl], u_h.astype(BF16), (((0,), (0,)), ((), ())),
                                                           preferred_element_type=F32)
                us.append(u_h)
            o = jnp.concatenate(os_, axis=0) + jnp.dot(qk, jnp.concatenate(us, axis=0).astype(BF16),
                                                       preferred_element_type=F32)
            o_ref = of_ref if d == 0 else ob_ref
            for h in range(H):
                o_ref[0, pl.ds(r0, C), h * DN_DV:(h + 1) * DN_DV] = o[h * C:(h + 1) * C].astype(o_ref.dtype)
        return carry

    lax.fori_loop(0, n_chunks, body, 0)


def delta_rule(k, v, q, g, beta, n_ctx):
    B, Lt, W = k.shape
    H, C = DN_HEADS, DN_CHUNK
    nc = Lt // C
    gch = g.reshape(B, nc, C, 2, H)
    g_f = jnp.cumsum(gch[:, :, :, 0], axis=2)
    g_b = jnp.flip(jnp.cumsum(jnp.flip(gch[:, :, :, 1], axis=2), axis=2), axis=2)
    gcum = jnp.stack([g_f, g_b], axis=3)
    gtot = jnp.broadcast_to(jnp.stack([g_f[:, :, -1:], g_b[:, :, :1]], axis=3), gcum.shape)
    gcol = gcum.reshape(B, Lt, 2 * H)
    glast = gtot.reshape(B, Lt, 2 * H)
    grow = jnp.transpose(gcum, (0, 3, 1, 4, 2)).reshape(B, 2, nc, H * C)
    bt = beta.reshape(B, Lt, 2 * H)
    seq = lambda w: pl.BlockSpec((1, Lt, w), lambda b: (b, 0, 0))
    kern = functools.partial(_delta_kernel, n_ctx_chunks=n_ctx // C, n_chunks=nc)
    return pl.pallas_call(
        kern,
        grid=(B,),
        in_specs=[seq(W), seq(W), seq(W), seq(2 * H), seq(2 * H), seq(2 * H),
                  pl.BlockSpec((1, 2, nc, H * C), lambda b: (b, 0, 0, 0))],
        out_specs=[seq(W), seq(W)],
        out_shape=[jax.ShapeDtypeStruct((B, Lt, W), BF16)] * 2,
        scratch_shapes=[pltpu.VMEM((2, H, DN_DK, DN_DV), F32)],
        compiler_params=pltpu.CompilerParams(vmem_limit_bytes=48 * 1024 * 1024),
        name="delta_rule",
    )(k, v, q, gcol, glast, bt, grow)


ROUTE_TILE = 512


def _route_kernel(h_ref, w_ref, b_ref, idx_ref, rank_ref, wt_ref, cnt_ref, carry_ref):
    E, G, tn = N_EXPERTS, N_GROUPS, h_ref.shape[0]
    per = E // G
    neg = -jnp.inf

    @pl.when(pl.program_id(0) == 0)
    def _():
        carry_ref[...] = jnp.zeros_like(carry_ref)

    logits = lax.dot_general(w_ref[...], h_ref[...], (((1,), (1,)), ((), ())), preferred_element_type=F32)
    scores = jax.nn.sigmoid(logits)
    sel = scores + b_ref[...]
    iota_p = lax.broadcasted_iota(jnp.int32, (per, tn), 0)
    gs = []
    for g in range(G):
        blk = sel[g * per:(g + 1) * per]
        m1 = jnp.max(blk, axis=0, keepdims=True)
        i1 = jnp.min(jnp.where(blk == m1, iota_p, per), axis=0, keepdims=True)
        m2 = jnp.max(jnp.where(iota_p == i1, neg, blk), axis=0, keepdims=True)
        gs.append(m1 + m2)
    gsel = jnp.concatenate(gs, axis=0)
    iota_g = lax.broadcasted_iota(jnp.int32, (G, tn), 0)
    gpick = jnp.zeros((G, tn), F32)
    for _ in range(TOPK_GROUPS):
        m = jnp.max(gsel, axis=0, keepdims=True)
        i = jnp.min(jnp.where(gsel == m, iota_g, G), axis=0, keepdims=True)
        hit = iota_g == i
        gpick = jnp.where(hit, 1.0, gpick)
        gsel = jnp.where(hit, neg, gsel)
    emask = jnp.concatenate([jnp.broadcast_to(gpick[g:g + 1], (per, tn)) for g in range(G)], axis=0) > 0.5
    cand = jnp.where(emask, sel, neg)
    iota_e = lax.broadcasted_iota(jnp.int32, (E, tn), 0)
    picked = jnp.zeros((E, tn), F32)
    ids, pick_scores = [], []
    for _ in range(TOP_K):
        m = jnp.max(cand, axis=0, keepdims=True)
        i = jnp.min(jnp.where(cand == m, iota_e, E), axis=0, keepdims=True)
        hit = iota_e == i
        ids.append(i)
        pick_scores.append(jnp.sum(jnp.where(hit, scores, 0.0), axis=0, keepdims=True))
        picked = jnp.where(hit, 1.0, picked)
        cand = jnp.where(hit, neg, cand)
    total = pick_scores[0]
    for sc in pick_scores[1:]:
        total = total + sc
    rr = lax.broadcasted_iota(jnp.int32, (tn, tn), 0)
    cc = lax.broadcasted_iota(jnp.int32, (tn, tn), 1)
    before = (rr < cc).astype(BF16)
    rank = jnp.dot(picked.astype(BF16), before, preferred_element_type=F32) + carry_ref[:, :1]
    ranks = [jnp.sum(jnp.where(iota_e == i, rank, 0.0), axis=0, keepdims=True) for i in ids]
    idx_ref[...] = jnp.concatenate(ids, axis=0)
    rank_ref[...] = jnp.concatenate(ranks, axis=0).astype(jnp.int32)
    wt_ref[...] = jnp.concatenate([sc / (total + 1e-20) * ROUTED_SCALE for sc in pick_scores], axis=0)
    carry_ref[...] = carry_ref[...] + jnp.sum(picked, axis=1, keepdims=True)
    cnt_ref[...] = carry_ref[...]


def route(h, w_router, b_router):
    T, D = h.shape
    tn = _pick_tile(T, ROUTE_TILE)
    pick = pl.BlockSpec((TOP_K, tn), lambda i: (0, i))
    idx, rank, wts, cnt = pl.pallas_call(
        _route_kernel,
        grid=(T // tn,),
        in_specs=[pl.BlockSpec((tn, D), lambda i: (i, 0)),
                  pl.BlockSpec((N_EXPERTS, D), lambda i: (0, 0)),
                  pl.BlockSpec((N_EXPERTS, 1), lambda i: (0, 0))],
        out_specs=[pick, pick, pick, pl.BlockSpec((N_EXPERTS, LANES), lambda i: (0, 0))],
        out_shape=[jax.ShapeDtypeStruct((TOP_K, T), jnp.int32), jax.ShapeDtypeStruct((TOP_K, T), jnp.int32),
                   jax.ShapeDtypeStruct((TOP_K, T), F32), jax.ShapeDtypeStruct((N_EXPERTS, LANES), F32)],
        scratch_shapes=[pltpu.VMEM((N_EXPERTS, LANES), F32)],
        compiler_params=pltpu.CompilerParams(dimension_semantics=("arbitrary",)),
        name="route",
    )(h, w_router.T.astype(BF16), b_router.astype(F32)[:, None])
    return idx, rank, wts, cnt[:, 0].astype(jnp.int32)


ROW_TILE = 256
HALO_ROWS = 16

OFF_Q = DN_QK_W + DN_V_W + 2 * ATTN_KV_W
OFF_Z = OFF_Q + DN_QK_W
OFF_U = OFF_Z + DN_V_W
OFF_B = OFF_U + SC_WIDTH
OFF_C = OFF_B + SC_WIDTH
OFF_ATQ = OFF_C + SC_WIDTH
OFF_G = OFF_ATQ + ATTN_Q_W


def _rms(x):
    return x * lax.rsqrt(jnp.mean(x * x, axis=-1, keepdims=True) + NORM_EPS)


def _merge_kernel(of_ref, ob_ref, z_ref, u_ref, b_ref, c_ref, up_ref, cp_ref, un_ref, cn_ref, at_ref,
                  g0_ref, g1_ref, g2_ref, h_ref, gate_ref, shift_ref, scale_ref, dnn_ref, wconv_ref, post_ref,
                  pre_ref, wdn_ref, wsc_ref, wat_ref, wout_ref, hn_ref, f_ref):
    i, n_i = pl.program_id(1), pl.num_programs(1)
    tm = h_ref.shape[1]
    o = of_ref[0].astype(F32) + ob_ref[0].astype(F32)
    z = z_ref[0].astype(F32)
    parts = []
    for h in range(DN_HEADS):
        cols = slice(h * DN_DV, (h + 1) * DN_DV)
        zh = z[:, cols]
        parts.append(_rms(o[:, cols]) * dnn_ref[...] * (zh * jax.nn.sigmoid(zh)))
    y_dn = jnp.dot(jnp.concatenate(parts, axis=1).astype(BF16), wdn_ref[...], preferred_element_type=F32)

    cu = c_ref[0].astype(F32) * u_ref[0].astype(F32)
    last = HALO_ROWS - 1
    prev_row = cp_ref[0, last:last + 1, :].astype(F32) * up_ref[0, last:last + 1, :].astype(F32)
    next_row = cn_ref[0, 0:1, :].astype(F32) * un_ref[0, 0:1, :].astype(F32)
    prev_row = jnp.where(i == 0, 0.0, prev_row)
    next_row = jnp.where(i == n_i - 1, 0.0, next_row)
    rows = lax.broadcasted_iota(jnp.int32, (tm, 1), 0)
    cu_prev = jnp.where(rows == 0, prev_row, pltpu.roll(cu, 1, 0))
    cu_next = jnp.where(rows == tm - 1, next_row, pltpu.roll(cu, tm - 1, 0))
    wc = wconv_ref[...]
    conv = cu_prev * wc[0:1] + cu * wc[1:2] + cu_next * wc[2:3]
    y_sc = jnp.dot((b_ref[0].astype(F32) * conv).astype(BF16), wsc_ref[...], preferred_element_type=F32)

    y_at = jnp.dot(at_ref[0], wat_ref[...], preferred_element_type=F32)
    comb = (jax.nn.sigmoid(g0_ref[0].astype(F32)) * y_dn + jax.nn.sigmoid(g1_ref[0].astype(F32)) * y_sc
            + jax.nn.sigmoid(g2_ref[0].astype(F32)) * y_at)
    y = jnp.dot(comb.astype(BF16), wout_ref[...], preferred_element_type=F32)
    hn = h_ref[0] + gate_ref[0] * (_rms(y) * post_ref[...])
    hn_ref[0] = hn
    f_ref[0] = (_rms(hn) * pre_ref[...] * (1.0 + scale_ref[0]) + shift_ref[0]).astype(f_ref.dtype)


def merge(o_f, o_b, o_row0, p, o_attn, h, gate, shift, scale, dn_norm, w_sc_conv, mix_post, ffn_pre,
          w_proj_dn, w_proj_sc, w_proj_attn, w_out):
    B, Ls, D = h.shape
    assert D == ATTN_Q_W and OFF_G % D == 0
    tm = _pick_tile(Ls, ROW_TILE)
    assert o_row0 % tm == 0 and tm % HALO_ROWS == 0
    r0 = o_row0 // tm
    hb = tm // HALO_ROWS
    n_halo = Ls // HALO_ROWS
    W = SC_WIDTH
    col = lambda off, w: pl.BlockSpec((1, tm, w), lambda b, i: (b, i, off // w))
    prev = lambda off: pl.BlockSpec((1, HALO_ROWS, W), lambda b, i: (b, jnp.maximum(i * hb - 1, 0), off // W))
    nxt = lambda off: pl.BlockSpec((1, HALO_ROWS, W), lambda b, i: (b, jnp.minimum((i + 1) * hb, n_halo - 1), off // W))
    o_spec = pl.BlockSpec((1, tm, DN_V_W), lambda b, i: (b, r0 + i, 0))
    mod = pl.BlockSpec((1, 1, D), lambda b, i: (b, 0, 0))
    full = lambda a: pl.BlockSpec(a.shape, lambda b, i: (0,) * a.ndim)
    row = pl.BlockSpec((1, tm, D), lambda b, i: (b, i, 0))
    vecs = [dn_norm.astype(F32)[None], w_sc_conv.astype(F32), mix_post.astype(F32)[None], ffn_pre.astype(F32)[None]]
    ws = [w_proj_dn.astype(BF16), w_proj_sc.astype(BF16), w_proj_attn.astype(BF16), w_out.astype(BF16)]
    return pl.pallas_call(
        _merge_kernel,
        grid=(B, Ls // tm),
        in_specs=[o_spec, o_spec, col(OFF_Z, W), col(OFF_U, W), col(OFF_B, W), col(OFF_C, W),
                  prev(OFF_U), prev(OFF_C), nxt(OFF_U), nxt(OFF_C), row,
                  col(OFF_G, D), col(OFF_G + D, D), col(OFF_G + 2 * D, D), row, mod, mod, mod]
                 + [full(a) for a in vecs] + [full(a) for a in ws],
        out_specs=[row, row],
        out_shape=[jax.ShapeDtypeStruct((B, Ls, D), F32), jax.ShapeDtypeStruct((B, Ls, D), BF16)],
        compiler_params=pltpu.CompilerParams(vmem_limit_bytes=48 * 1024 * 1024),
        name="merge",
    )(o_f, o_b, p, p, p, p, p, p, p, p, o_attn, p, p, p, h, gate, shift, scale, *vecs, *ws)


def _combine_kernel(*refs, with_next):
    yg_ref, wt_ref, f_ref, wsg_ref, wsu_ref, wsd_ref, h_ref, gate_ref, post_ref = refs[:9]
    wt = wt_ref[...]
    acc = yg_ref[0].astype(F32) * wt[:, 0:1]
    for k in range(1, TOP_K):
        acc = acc + yg_ref[k].astype(F32) * wt[:, k:k + 1]
    x = f_ref[...]
    g = jnp.dot(x, wsg_ref[...], preferred_element_type=F32)
    u = jnp.dot(x, wsu_ref[...], preferred_element_type=F32)
    a = (g * jax.nn.sigmoid(g) * u).astype(BF16)
    y = acc + jnp.dot(a, wsd_ref[...], preferred_element_type=F32)
    hn = h_ref[...] + gate_ref[0] * (_rms(y) * post_ref[...])
    if with_next:
        pre_ref, scale_ref, shift_ref, hn_ref, nx_ref = refs[9:]
        nx_ref[...] = (_rms(hn) * pre_ref[...] * (1.0 + scale_ref[0]) + shift_ref[0]).astype(nx_ref.dtype)
    else:
        hn_ref, = refs[9:]
    hn_ref[...] = hn


def combine(yg, wt, f, tok0, ws_gate, ws_up, ws_down, h, gate, ffn_post, nxt=None):
    B, Ls, D = h.shape
    tm = _pick_tile(Ls, ROW_TILE)
    assert tok0 % tm == 0
    t0 = tok0 // tm
    per_b = Ls // tm
    F = ws_gate.shape[-1]
    tok = pl.BlockSpec((tm, D), lambda i: (t0 + i, 0))
    row = pl.BlockSpec((tm, D), lambda i: (i, 0))
    mod = pl.BlockSpec((1, 1, D), lambda i: (i // per_b, 0, 0))
    vec = pl.BlockSpec((1, D), lambda i: (0, 0))
    in_specs = [pl.BlockSpec((TOP_K, tm, D), lambda i: (0, t0 + i, 0)),
                pl.BlockSpec((tm, TOP_K), lambda i: (t0 + i, 0)), tok,
                pl.BlockSpec((D, F), lambda i: (0, 0)), pl.BlockSpec((D, F), lambda i: (0, 0)),
                pl.BlockSpec((F, D), lambda i: (0, 0)), row, mod, vec]
    args = [yg, wt, f, ws_gate.astype(BF16), ws_up.astype(BF16), ws_down.astype(BF16), h.reshape(B * Ls, D), gate,
            ffn_post.astype(F32)[None]]
    out_specs = [row]
    out_shape = [jax.ShapeDtypeStruct((B * Ls, D), F32)]
    if nxt is not None:
        in_specs += [vec, mod, mod]
        args += [nxt[0].astype(F32)[None], nxt[1], nxt[2]]
        out_specs.append(row)
        out_shape.append(jax.ShapeDtypeStruct((B * Ls, D), BF16))
    outs = pl.pallas_call(
        functools.partial(_combine_kernel, with_next=nxt is not None),
        grid=(B * Ls // tm,),
        in_specs=in_specs,
        out_specs=out_specs,
        out_shape=out_shape,
        compiler_params=pltpu.CompilerParams(vmem_limit_bytes=48 * 1024 * 1024),
        name="combine",
    )(*args)
    return (outs[0].reshape(B, Ls, D), outs[1] if nxt is not None else None)


def _split(x, sizes):
    cuts = [int(v) for v in np.cumsum(sizes)[:-1]]
    return jnp.split(x, cuts, axis=-1)


def rmsnorm(x, w):
    xf = x.astype(F32)
    y = xf * lax.rsqrt(jnp.mean(xf * xf, axis=-1, keepdims=True) + NORM_EPS)
    return y * w.astype(F32)


def l2norm(x):
    return x * lax.rsqrt(jnp.sum(x * x, axis=-1, keepdims=True) + NORM_EPS)


def dwconv3(x, w):
    xf = x.astype(F32)
    prev = jnp.pad(xf, ((0, 0), (1, 0), (0, 0)))[:, :-1]
    nxt = jnp.pad(xf, ((0, 0), (0, 1), (0, 0)))[:, 1:]
    return prev * w[0] + xf * w[1] + nxt * w[2]


def rope_tables(rows):
    row_pos = jnp.repeat(jnp.arange(rows, dtype=F32), GRID_W)
    col_pos = jnp.tile(jnp.arange(GRID_W, dtype=F32), rows)
    axis_dim = ATTN_HD // 2
    inv_freq = ROPE_THETA ** (-jnp.arange(0, axis_dim, 2, dtype=F32) / axis_dim)
    ang = jnp.concatenate([row_pos[:, None] * inv_freq, col_pos[:, None] * inv_freq], axis=-1)
    return jnp.cos(ang), jnp.sin(ang)


def rope_halves(x, cos, sin):
    half = ATTN_HD // 2
    x1, x2 = x[..., :half], x[..., half:]
    cs, sn = cos[None, :, None, :], sin[None, :, None, :]
    return jnp.concatenate([x1 * cs - x2 * sn, x1 * sn + x2 * cs], axis=-1)


def _deinterleave_perm(n_heads):
    base = np.concatenate([np.arange(0, ATTN_HD, 2), np.arange(1, ATTN_HD, 2)])
    return np.concatenate([h * ATTN_HD + base for h in range(n_heads)])


def prep_w_in(w_in):
    D = w_in.shape[0]
    o_kv = 0
    o_a = DN_QK_W + DN_V_W
    o_atk = o_a + 4 * DN_HEADS
    o_atv = o_atk + ATTN_KV_W
    o_rest = o_atv + ATTN_KV_W
    o_atq = o_rest + DN_QK_W + DN_V_W + 3 * SC_WIDTH
    o_gates = o_atq + ATTN_Q_W
    w_kv = w_in[:, o_kv:o_a]
    w_ab = w_in[:, o_a:o_atk]
    w_atk = w_in[:, o_atk:o_atv][:, _deinterleave_perm(ATTN_KV_HEADS)]
    w_atv = w_in[:, o_atv:o_rest]
    w_pre_q = w_in[:, o_rest:o_atq]
    w_atq = w_in[:, o_atq:o_gates][:, _deinterleave_perm(ATTN_HEADS)]
    w_gates = w_in[:, o_gates:]
    w_main = jnp.concatenate([w_kv, w_atk, w_atv, w_pre_q, w_atq, w_gates], axis=1).astype(BF16)
    w_ab = jnp.pad(w_ab, ((0, 0), (0, LANES - w_ab.shape[1]))).astype(BF16)
    return w_main, w_ab


KV_MAIN = DN_QK_W + DN_V_W + 2 * ATTN_KV_W


def kv_side(p_kv, p_ab, w_conv_kv, dn_a_log, dn_dt_bias, k_norm_p):
    B, L = p_kv.shape[:2]
    dn_kv, at_k, at_v = _split(p_kv, (DN_QK_W + DN_V_W, ATTN_KV_W, ATTN_KV_W))
    dn_kv = jax.nn.silu(dwconv3(dn_kv, w_conv_kv))
    dn_k = l2norm(dn_kv[..., :DN_QK_W].reshape(B, L, DN_HEADS, DN_DK))
    dn_v = dn_kv[..., DN_QK_W:].reshape(B, L, DN_HEADS, DN_DV)
    a = p_ab[..., :2 * DN_HEADS].astype(F32).reshape(B, L, 2, DN_HEADS)
    dn_g = -jnp.exp(dn_a_log.astype(F32)) * jax.nn.softplus(a + dn_dt_bias.astype(F32))
    dn_beta = jax.nn.sigmoid(p_ab[..., 2 * DN_HEADS:4 * DN_HEADS].astype(F32).reshape(B, L, 2, DN_HEADS))
    at_k = rmsnorm(at_k.reshape(B, L, ATTN_KV_HEADS, ATTN_HD), k_norm_p)
    return dn_k, dn_v, dn_g, dn_beta, at_k, at_v


def q_side(p, w_conv_q, q_norm_p):
    B, L = p.shape[:2]
    dn_q = jax.nn.silu(dwconv3(p[..., OFF_Q:OFF_Q + DN_QK_W], w_conv_q))
    dn_q = l2norm(dn_q.reshape(B, L, DN_HEADS, DN_DK)) * (DN_DK ** -0.5)
    at_q = rmsnorm(p[..., OFF_ATQ:OFF_ATQ + ATTN_Q_W].reshape(B, L, ATTN_HEADS, ATTN_HD), q_norm_p)
    return dn_q, at_q


def token_mixers(hl, hc, w_in, w_dn_conv_q, w_dn_conv_kv, dn_a_log, dn_dt_bias, q_norm, k_norm, cos, sin, ctx_out):
    L = cos.shape[0]
    B = hl.shape[0] // L
    Lc = hc.shape[0] // B
    w_main, w_ab = prep_w_in(w_in)
    base = np.concatenate([np.arange(0, ATTN_HD, 2), np.arange(1, ATTN_HD, 2)])
    q_norm_p, k_norm_p = q_norm[base], k_norm[base]

    p_l = matmul(hl, w_main, BF16).reshape(B, L, -1)
    ab_l = matmul(hl, w_ab, F32).reshape(B, L, -1)
    p_c = matmul(hc, w_main if ctx_out else w_main[:, :KV_MAIN], BF16).reshape(B, Lc, -1)
    ab_c = matmul(hc, w_ab, F32).reshape(B, Lc, -1)

    dn_kl, dn_vl, dn_gl, dn_bl, at_kl, at_vl = kv_side(p_l[..., :KV_MAIN], ab_l, w_dn_conv_kv, dn_a_log, dn_dt_bias, k_norm_p)
    dn_kc, dn_vc, dn_gc, dn_bc, at_kc, at_vc = kv_side(p_c[..., :KV_MAIN], ab_c, w_dn_conv_kv, dn_a_log, dn_dt_bias, k_norm_p)
    dn_ql, at_ql = q_side(p_l, w_dn_conv_q, q_norm_p)
    at_ql = rope_halves(at_ql, cos, sin)
    at_kl = rope_halves(at_kl, cos, sin)
    if ctx_out:
        dn_qc, at_qc = q_side(p_c, w_dn_conv_q, q_norm_p)
    else:
        dn_qc = jnp.zeros((B, Lc, DN_HEADS, DN_DK), F32)

    cat = lambda c_part, l_part: jnp.concatenate(
        [c_part.reshape(B, Lc, -1), l_part.reshape(B, L, -1)], axis=1).astype(BF16)
    o_f, o_b = delta_rule(cat(dn_kc, dn_kl), cat(dn_vc, dn_vl), cat(dn_qc, dn_ql),
                          jnp.concatenate([dn_gc, dn_gl], axis=1), jnp.concatenate([dn_bc, dn_bl], axis=1), Lc)

    scale = ATTN_HD ** -0.5
    k_all = cat(at_kc, at_kl)
    v_all = cat(at_vc, at_vl)
    o_at_l = attention((at_ql * scale).reshape(B, L, ATTN_Q_W), k_all, v_all)
    o_at_c = None
    if ctx_out:
        o_at_c = attention((at_qc * scale).reshape(B, Lc, ATTN_Q_W), k_all[:, :Lc], v_all[:, :Lc])
    return p_l, p_c, o_f, o_b, o_at_l, o_at_c


def moe_routed(f, w_router, b_router, w_gate, w_up, w_down):
    T, D = f.shape
    idx, rank, wts, counts = route(f, w_router, b_router)
    n_blocks = -(-(T * TOP_K) // MOE_BLOCK) + N_EXPERTS
    n_slots = n_blocks * MOE_BLOCK
    padded = (counts + MOE_BLOCK - 1) // MOE_BLOCK * MOE_BLOCK
    pad_end = jnp.cumsum(padded)
    pad_start = pad_end - padded
    experts = jnp.arange(N_EXPERTS, dtype=jnp.int32)
    dest = jnp.sum(jnp.where(idx[..., None] == experts, pad_start, 0), axis=-1) + rank
    tok = jnp.broadcast_to(jnp.arange(T, dtype=jnp.int32)[None], (TOP_K, T))
    slot_tok = jnp.full((n_slots,), T, jnp.int32).at[dest.reshape(-1)].set(tok.reshape(-1))
    block_start = jnp.arange(n_blocks, dtype=jnp.int32) * MOE_BLOCK
    block_e = jnp.minimum(jnp.sum(pad_end[None, :] <= block_start[:, None], axis=1), N_EXPERTS - 1)
    f_pad = jnp.concatenate([f, jnp.zeros((1, D), BF16)], axis=0)
    y_sorted = expert_blocks(f_pad[slot_tok], block_e, w_gate.astype(BF16), w_up.astype(BF16), w_down.astype(BF16))
    return y_sorted[dest], wts.T


def kernel(x, c, ctx, c_ctx, w_mod, b_mod, mix_pre, mix_post, ffn_pre, ffn_post, w_in, w_dn_conv_q, w_dn_conv_kv, dn_a_log, dn_dt_bias, dn_norm, w_sc_conv, q_norm, k_norm, w_proj_dn, w_proj_sc, w_proj_attn, w_out, w_router, b_router, w_exp_gate, w_exp_up, w_exp_down, w_sh_gate, w_sh_up, w_sh_down):
    B, L, D = x.shape
    Lc = ctx.shape[1]
    depth = w_in.shape[0]
    cos, sin = rope_tables(L // GRID_W)
    silu_all = jax.nn.silu(jnp.concatenate([c, c_ctx[None]], axis=0))
    n_pad = -(-(B + 1) // 8) * 8
    silu_all = jnp.pad(silu_all, ((0, n_pad - (B + 1)), (0, 0)))
    mods = [matmul(silu_all, w_mod[layer], F32) + b_mod[layer] for layer in range(depth)]
    lat_mod = lambda layer, j: mods[layer][:B, None, j * D:(j + 1) * D]
    ctx_mod = lambda layer, j: jnp.broadcast_to(mods[layer][B, j * D:(j + 1) * D], (B, 1, D))

    h_lat, h_ctx = x, ctx
    hl = (rmsnorm(x, mix_pre[0]) * (1 + lat_mod(0, 1)) + lat_mod(0, 0)).astype(BF16).reshape(B * L, D)
    hc = (rmsnorm(ctx, mix_pre[0]) * (1 + ctx_mod(0, 1)) + ctx_mod(0, 0)).astype(BF16).reshape(B * Lc, D)
    for layer in range(depth):
        ctx_out = layer < depth - 1
        p_l, p_c, o_f, o_b, o_at_l, o_at_c = token_mixers(
            hl, hc, w_in[layer], w_dn_conv_q[layer], w_dn_conv_kv[layer], dn_a_log[layer], dn_dt_bias[layer],
            q_norm[layer], k_norm[layer], cos, sin, ctx_out)
        merge_w = (dn_norm[layer], w_sc_conv[layer], mix_post[layer], ffn_pre[layer],
                   w_proj_dn[layer], w_proj_sc[layer], w_proj_attn[layer], w_out[layer])
        h_lat, f_l = merge(o_f, o_b, Lc, p_l, o_at_l, h_lat, lat_mod(layer, 2), lat_mod(layer, 3), lat_mod(layer, 4),
                           *merge_w)
        f = f_l.reshape(B * L, D)
        if ctx_out:
            h_ctx, f_c = merge(o_f, o_b, 0, p_c, o_at_c, h_ctx, ctx_mod(layer, 2), ctx_mod(layer, 3),
                               ctx_mod(layer, 4), *merge_w)
            f = jnp.concatenate([f, f_c.reshape(B * Lc, D)], axis=0)
        yg, wt = moe_routed(f, w_router[layer], b_router[layer], w_exp_gate[layer], w_exp_up[layer], w_exp_down[layer])
        shared_w = (w_sh_gate[layer], w_sh_up[layer], w_sh_down[layer])
        nxt_l = nxt_c = None
        if ctx_out:
            nxt_l = (mix_pre[layer + 1], lat_mod(layer + 1, 1), lat_mod(layer + 1, 0))
            nxt_c = (mix_pre[layer + 1], ctx_mod(layer + 1, 1), ctx_mod(layer + 1, 0))
            h_ctx, hc = combine(yg, wt, f, B * L, *shared_w, h_ctx, ctx_mod(layer, 5), ffn_post[layer], nxt_c)
        h_lat, hl = combine(yg, wt, f, 0, *shared_w, h_lat, lat_mod(layer, 5), ffn_post[layer], nxt_l)
    return h_lat
```

```python
import functools

import jax
import jax.numpy as jnp
import numpy as np
from jax import lax
from jax.experimental import pallas as pl
from jax.experimental.pallas import tpu as pltpu

F32 = jnp.float32
BF16 = jnp.bfloat16

GRID_W = 64
NORM_EPS = 1e-6
DN_HEADS = 4
DN_DK = 128
DN_DV = 128
DN_CHUNK = 64
SC_WIDTH = 512
ATTN_HEADS = 8
ATTN_KV_HEADS = 2
ATTN_GROUP = ATTN_HEADS // ATTN_KV_HEADS
ATTN_HD = 128
ROPE_THETA = 10000.0
N_EXPERTS = 64
TOP_K = 8
N_GROUPS = 8
TOPK_GROUPS = 4
ROUTED_SCALE = 2.5
MOE_BLOCK = 256
N_BRANCH = 3

DN_QK_W = DN_HEADS * DN_DK
DN_V_W = DN_HEADS * DN_DV
ATTN_Q_W = ATTN_HEADS * ATTN_HD
ATTN_KV_W = ATTN_KV_HEADS * ATTN_HD
LANES = 128


def _mm_kernel(x_ref, w_ref, o_ref):
    o_ref[...] = jnp.dot(x_ref[...], w_ref[...], preferred_element_type=F32).astype(o_ref.dtype)


def _pick_tile(n, want):
    t = min(n, want)
    while n % t:
        t //= 2
    return t


def matmul(x, w, out_dtype, tm=512, tn=1024):
    M, K = x.shape
    N = w.shape[1]
    tm = _pick_tile(M, tm)
    tn = _pick_tile(N, tn)
    return pl.pallas_call(
        _mm_kernel,
        grid=(N // tn, M // tm),
        in_specs=[pl.BlockSpec((tm, K), lambda j, i: (i, 0)),
                  pl.BlockSpec((K, tn), lambda j, i: (0, j))],
        out_specs=pl.BlockSpec((tm, tn), lambda j, i: (i, j)),
        out_shape=jax.ShapeDtypeStruct((M, N), out_dtype),
        name="matmul",
    )(x.astype(BF16), w.astype(BF16))


OFF_Q = DN_QK_W + DN_V_W + 2 * ATTN_KV_W
OFF_Z = OFF_Q + DN_QK_W
OFF_U = OFF_Z + DN_V_W
OFF_B = OFF_U + SC_WIDTH
OFF_C = OFF_B + SC_WIDTH
OFF_ATQ = OFF_C + SC_WIDTH
OFF_G = OFF_ATQ + ATTN_Q_W


def _rms(x):
    return x * lax.rsqrt(jnp.mean(x * x, axis=-1, keepdims=True) + NORM_EPS)


K_COL0 = (DN_QK_W + DN_V_W) // ATTN_HD
V_COL0 = (DN_QK_W + DN_V_W + ATTN_KV_W) // ATTN_HD


def _head_norm_rope(x, w, cos2, sin2):
    y = _rms(x) * w
    if cos2 is None:
        return y
    return y * cos2 + pltpu.roll(y, ATTN_HD // 2, 1) * sin2


def _attn_kernel(*refs, n_ctx, n_lat, rope_q):
    if n_lat:
        q_ref, qn_ref, kn_ref, cq_ref, sq_ref, kc_ref, vc_ref, kl_ref, vl_ref, ck_ref, sk_ref, o_ref, k_s, v_s = refs
    else:
        q_ref, qn_ref, kn_ref, kc_ref, vc_ref, o_ref, k_s, v_s = refs

    @pl.when(pl.program_id(2) == 0)
    def _():
        k_s[0:n_ctx, :] = _head_norm_rope(kc_ref[0].astype(F32), kn_ref[...], None, None).astype(BF16)
        v_s[0:n_ctx, :] = vc_ref[0]
        if n_lat:
            k_s[n_ctx:n_ctx + n_lat, :] = _head_norm_rope(kl_ref[0].astype(F32), kn_ref[...], ck_ref[...],
                                                          sk_ref[...]).astype(BF16)
            v_s[n_ctx:n_ctx + n_lat, :] = vl_ref[0]

    k, v = k_s[...], v_s[...]
    cos2 = cq_ref[...] if rope_q else None
    sin2 = sq_ref[...] if rope_q else None
    for g in range(ATTN_GROUP):
        cols = slice(g * ATTN_HD, (g + 1) * ATTN_HD)
        q = _head_norm_rope(q_ref[0, :, cols].astype(F32), qn_ref[...], cos2, sin2).astype(BF16)
        s = lax.dot_general(q, k, (((1,), (1,)), ((), ())), preferred_element_type=F32)
        m = jnp.max(s, axis=-1, keepdims=True)
        p = jnp.exp(s - m)
        l = jnp.sum(p, axis=-1, keepdims=True)
        o = jnp.dot(p.astype(BF16), v, preferred_element_type=F32)
        o_ref[0, :, cols] = (o / l).astype(o_ref.dtype)


def attention(p_q, p_c, p_l, q_norm_s, k_norm_p, cos2, sin2, tq=256):
    B, Lq, _ = p_q.shape
    Lc = p_c.shape[1]
    L = 0 if p_l is None else p_l.shape[1]
    tq = _pick_tile(Lq, tq)
    gw = ATTN_GROUP * ATTN_HD
    q_spec = pl.BlockSpec((1, tq, gw), lambda b, h, i: (b, i, OFF_ATQ // gw + h))
    vec = pl.BlockSpec((1, ATTN_HD), lambda b, h, i: (0, 0))
    kv = lambda n, c0: pl.BlockSpec((1, n, ATTN_HD), lambda b, h, i: (b, 0, c0 + h))
    in_specs = [q_spec, vec, vec]
    args = [p_q, q_norm_s[None], k_norm_p[None]]
    if L:
        in_specs += [pl.BlockSpec((tq, ATTN_HD), lambda b, h, i: (i, 0))] * 2
        args += [cos2, sin2]
    in_specs += [kv(Lc, K_COL0), kv(Lc, V_COL0)]
    args += [p_c, p_c]
    if L:
        in_specs += [kv(L, K_COL0), kv(L, V_COL0), pl.BlockSpec((L, ATTN_HD), lambda b, h, i: (0, 0)),
                     pl.BlockSpec((L, ATTN_HD), lambda b, h, i: (0, 0))]
        args += [p_l, p_l, cos2, sin2]
    return pl.pallas_call(
        functools.partial(_attn_kernel, n_ctx=Lc, n_lat=L, rope_q=bool(L)),
        grid=(B, ATTN_KV_HEADS, Lq // tq),
        in_specs=in_specs,
        out_specs=pl.BlockSpec((1, tq, gw), lambda b, h, i: (b, i, h)),
        out_shape=jax.ShapeDtypeStruct((B, Lq, ATTN_Q_W), BF16),
        scratch_shapes=[pltpu.VMEM((Lc + L, ATTN_HD), BF16), pltpu.VMEM((Lc + L, ATTN_HD), BF16)],
        compiler_params=pltpu.CompilerParams(dimension_semantics=("arbitrary", "arbitrary", "arbitrary")),
        name="attention",
    )(*args)


def _expert_kernel(be_ref, x_ref, wg_ref, wu_ref, wd_ref, o_ref, wg_s, wu_s, wd_s):
    i = pl.program_id(0)

    @pl.when((i == 0) | (be_ref[i] != be_ref[jnp.maximum(i - 1, 0)]))
    def _():
        wg_s[...] = wg_ref[0].astype(BF16)
        wu_s[...] = wu_ref[0].astype(BF16)
        wd_s[...] = wd_ref[0].astype(BF16)

    x = x_ref[...]
    g = jnp.dot(x, wg_s[...], preferred_element_type=F32)
    u = jnp.dot(x, wu_s[...], preferred_element_type=F32)
    a = (g * jax.nn.sigmoid(g) * u).astype(BF16)
    o_ref[...] = jnp.dot(a, wd_s[...], preferred_element_type=F32).astype(o_ref.dtype)


def expert_blocks(x_sorted, block_e, wg, wu, wd, tm=MOE_BLOCK):
    n_rows, D = x_sorted.shape
    F = wg.shape[-1]
    n_blocks = n_rows // tm
    grid_spec = pltpu.PrefetchScalarGridSpec(
        num_scalar_prefetch=1,
        grid=(n_blocks,),
        in_specs=[pl.BlockSpec((tm, D), lambda i, be: (i, 0)),
                  pl.BlockSpec((1, D, F), lambda i, be: (be[i], 0, 0)),
                  pl.BlockSpec((1, D, F), lambda i, be: (be[i], 0, 0)),
                  pl.BlockSpec((1, F, D), lambda i, be: (be[i], 0, 0))],
        out_specs=pl.BlockSpec((tm, D), lambda i, be: (i, 0)),
        scratch_shapes=[pltpu.VMEM((D, F), BF16), pltpu.VMEM((D, F), BF16), pltpu.VMEM((F, D), BF16)],
    )
    return pl.pallas_call(
        _expert_kernel,
        grid_spec=grid_spec,
        out_shape=jax.ShapeDtypeStruct((n_rows, D), BF16),
        compiler_params=pltpu.CompilerParams(dimension_semantics=("arbitrary",)),
        name="expert_blocks",
    )(block_e.astype(jnp.int32), x_sorted, wg, wu, wd)


DN_STACK = DN_HEADS * DN_CHUNK
DN_UNROLL = 2
DN_SPLIT_STEPS = 3
CONV_HALO = 16


def _conv_silu(x_ref, w_ref, r0, n_rows):
    C = DN_CHUNK
    x = x_ref[0, pl.ds(r0, C), :].astype(F32)
    lo = pl.multiple_of(jnp.maximum(r0 - CONV_HALO, 0), CONV_HALO)
    hi = pl.multiple_of(jnp.minimum(r0 + C, n_rows - CONV_HALO), CONV_HALO)
    before = x_ref[0, pl.ds(lo, CONV_HALO), :][CONV_HALO - 1:CONV_HALO].astype(F32)
    after = x_ref[0, pl.ds(hi, CONV_HALO), :][0:1].astype(F32)
    before = jnp.where(r0 > 0, before, 0.0)
    after = jnp.where(r0 + C < n_rows, after, 0.0)
    rows = lax.broadcasted_iota(jnp.int32, (C, 1), 0)
    x_prev = jnp.where(rows == 0, before, pltpu.roll(x, 1, 0))
    x_next = jnp.where(rows == C - 1, after, pltpu.roll(x, C - 1, 0))
    w = w_ref[...]
    y = x_prev * w[0:1] + x * w[1:2] + x_next * w[2:3]
    return y * jax.nn.sigmoid(y)


def _stack_heads(x, heads, l2_scale=None):
    parts = []
    for h in heads:
        xh = x[:, h * DN_DK:(h + 1) * DN_DK]
        if l2_scale is not None:
            xh = xh * (lax.rsqrt(jnp.sum(xh * xh, axis=-1, keepdims=True) + NORM_EPS) * l2_scale)
        parts.append(xh)
    return jnp.concatenate(parts, axis=0)


def _delta_kernel(*refs, n_ctx, n_lat, ctx_q):
    if ctx_q:
        (kc_ref, vc_ref, qc_ref, kl_ref, vl_ref, ql_ref, wk_ref, wv_ref, wq_ref,
         gcol_ref, glast_ref, beta_ref, grow_ref, of_ref, ob_ref, s_ref) = refs
    else:
        (kc_ref, vc_ref, kl_ref, vl_ref, ql_ref, wk_ref, wv_ref, wq_ref,
         gcol_ref, glast_ref, beta_ref, grow_ref, of_ref, ob_ref, s_ref) = refs
        qc_ref = None
    C, R, H = DN_CHUNK, DN_STACK, DN_HEADS
    s_ref[...] = jnp.zeros_like(s_ref)
    row = lax.broadcasted_iota(jnp.int32, (R, R), 0)
    col = lax.broadcasted_iota(jnp.int32, (R, R), 1)
    same_head = (row // C) == (col // C)
    strict = (same_head & (col < row), same_head & (col > row))
    eye = (row == col).astype(F32)

    def make_body(k_ref, v_ref, q_ref, n_rows, row0, unroll):
        n_seq = n_rows // C
        heads = range(H)

        def prepare(d, c):
            r_in = pl.multiple_of(c * C, C)
            r0 = pl.multiple_of(row0 + c * C, C)
            ks = _stack_heads(_conv_silu(k_ref, wk_ref, r_in, n_rows), heads, 1.0)
            vs = _stack_heads(_conv_silu(v_ref, wv_ref, r_in, n_rows), heads)
            if q_ref is None:
                qs = jnp.zeros((R, DN_DK), F32)
            else:
                qs = _stack_heads(_conv_silu(q_ref, wq_ref, r_in, n_rows), heads, DN_DK ** -0.5)

            def col_stack(ref):
                blk = ref[0, pl.ds(r0, C), :]
                return jnp.concatenate([blk[:, d * H + h:d * H + h + 1] for h in heads], axis=0)

            gc, gl, bt = col_stack(gcol_ref), col_stack(glast_ref), col_stack(beta_ref)
            gr = grow_ref[0, d, pl.ds(row0 // C + c, 1), :]
            ks_b = ks.astype(BF16)
            kb = ks * bt
            decay = jnp.exp(jnp.where(strict[d], gc - gr, -jnp.inf))
            g_kk = lax.dot_general(kb.astype(BF16), ks_b, (((1,), (1,)), ((), ())), preferred_element_type=F32)
            g_qk = lax.dot_general(qs.astype(BF16), ks_b, (((1,), (1,)), ((), ())), preferred_element_type=F32)
            e_gc = jnp.exp(gc)
            return dict(d=d, r0=r0, gl=gl,
                        p=(-(g_kk * decay)).astype(BF16),
                        qk=(g_qk * (decay + eye)).astype(BF16),
                        x=jnp.concatenate([vs * bt, kb * e_gc], axis=1),
                        kdec=(ks * jnp.exp(gl - gc)).astype(BF16),
                        qg=(qs * e_gc).astype(BF16))

        def body(n, carry):
            chains = []
            for u in range(unroll):
                m = n * unroll + u
                for d in range(2):
                    chains.append(prepare(d, m if d == 0 else n_seq - 1 - m))
            for it in range(6):
                for ch in chains:
                    x = ch["x"]
                    x_hi = x.astype(BF16)
                    step = jnp.dot(ch["p"], x_hi, preferred_element_type=F32)
                    if it < DN_SPLIT_STEPS:
                        x_lo = (x - x_hi.astype(F32)).astype(BF16)
                        step = step + jnp.dot(ch["p"], x_lo, preferred_element_type=F32)
                    ch["x"] = x + step
                if it < 5:
                    for ch in chains:
                        ch["p"] = jnp.dot(ch["p"], ch["p"], preferred_element_type=F32).astype(BF16)
            for u in range(unroll):
                pair = chains[2 * u:2 * u + 2]
                us, os_ = [[] for _ in pair], [[] for _ in pair]
                for h in heads:
                    sl = slice(h * C, (h + 1) * C)
                    for ci, ch in enumerate(pair):
                        d, gl = ch["d"], ch["gl"]
                        s_old = s_ref[d, h]
                        s_b = s_old.astype(BF16)
                        u_h = ch["x"][sl, :DN_DV] - jnp.dot(ch["x"][sl, DN_DV:].astype(BF16), s_b,
                                                            preferred_element_type=F32)
                        os_[ci].append(jnp.dot(ch["qg"][sl], s_b, preferred_element_type=F32))
                        cd = jnp.exp(gl[h * C:h * C + 1, :])
                        s_ref[d, h] = s_old * cd + lax.dot_general(ch["kdec"][sl], u_h.astype(BF16),
                                                                   (((0,), (0,)), ((), ())), preferred_element_type=F32)
                        us[ci].append(u_h)
                for ci, ch in enumerate(pair):
                    o = jnp.concatenate(os_[ci], axis=0) + jnp.dot(ch["qk"], jnp.concatenate(us[ci], axis=0).astype(BF16),
                                                                   preferred_element_type=F32)
                    o_ref = of_ref if ch["d"] == 0 else ob_ref
                    for h in heads:
                        o_ref[0, pl.ds(ch["r0"], C), h * DN_DV:(h + 1) * DN_DV] = o[h * C:(h + 1) * C].astype(o_ref.dtype)
            return carry

        return body, n_seq // unroll

    for k_ref, v_ref, q_ref, n_rows, row0 in ((kc_ref, vc_ref, qc_ref, n_ctx, 0), (kl_ref, vl_ref, ql_ref, n_lat, n_ctx)):
        unroll = DN_UNROLL if (n_rows // C) % DN_UNROLL == 0 else 1
        body, trips = make_body(k_ref, v_ref, q_ref, n_rows, row0, unroll)
        lax.fori_loop(0, trips, body, 0)


def delta_rule(p_c, p_l, w_conv_kv, w_conv_q, g, beta, ctx_q):
    B, Lc = p_c.shape[:2]
    L = p_l.shape[1]
    Lt = Lc + L
    H, C, W = DN_HEADS, DN_CHUNK, DN_QK_W
    nc = Lt // C
    gch = g.reshape(B, nc, C, 2, H)
    g_f = jnp.cumsum(gch[:, :, :, 0], axis=2)
    g_b = jnp.flip(jnp.cumsum(jnp.flip(gch[:, :, :, 1], axis=2), axis=2), axis=2)
    gcum = jnp.stack([g_f, g_b], axis=3)
    gtot = jnp.broadcast_to(jnp.stack([g_f[:, :, -1:], g_b[:, :, :1]], axis=3), gcum.shape)
    gcol = gcum.reshape(B, Lt, 2 * H)
    glast = gtot.reshape(B, Lt, 2 * H)
    grow = jnp.transpose(gcum, (0, 3, 1, 4, 2)).reshape(B, 2, nc, H * C)
    bt = beta.reshape(B, Lt, 2 * H)
    colblk = lambda n, j: pl.BlockSpec((1, n, W), lambda b: (b, 0, j))
    seq = lambda w: pl.BlockSpec((1, Lt, w), lambda b: (b, 0, 0))
    full = lambda a: pl.BlockSpec(a.shape, lambda b: (0,) * a.ndim)
    wk, wv, wq = (w_conv_kv[:, :W].astype(F32), w_conv_kv[:, W:].astype(F32), w_conv_q.astype(F32))
    q_col = OFF_Q // W
    in_specs = [colblk(Lc, 0), colblk(Lc, 1)] + ([colblk(Lc, q_col)] if ctx_q else [])
    args = [p_c, p_c] + ([p_c] if ctx_q else [])
    in_specs += [colblk(L, 0), colblk(L, 1), colblk(L, q_col), full(wk), full(wv), full(wq),
                 seq(2 * H), seq(2 * H), seq(2 * H), pl.BlockSpec((1, 2, nc, H * C), lambda b: (b, 0, 0, 0))]
    args += [p_l, p_l, p_l, wk, wv, wq, gcol, glast, bt, grow]
    return pl.pallas_call(
        functools.partial(_delta_kernel, n_ctx=Lc, n_lat=L, ctx_q=ctx_q),
        grid=(B,),
        in_specs=in_specs,
        out_specs=[seq(W), seq(W)],
        out_shape=[jax.ShapeDtypeStruct((B, Lt, W), BF16)] * 2,
        scratch_shapes=[pltpu.VMEM((2, H, DN_DK, DN_DV), F32)],
        compiler_params=pltpu.CompilerParams(vmem_limit_bytes=48 * 1024 * 1024),
        name="delta_rule",
    )(*args)


ROUTE_TILE = 512


def _route_kernel(h_ref, w_ref, b_ref, idx_ref, rank_ref, wt_ref, cnt_ref, carry_ref):
    E, G, tn = N_EXPERTS, N_GROUPS, h_ref.shape[0]
    per = E // G
    neg = -jnp.inf

    @pl.when(pl.program_id(0) == 0)
    def _():
        carry_ref[...] = jnp.zeros_like(carry_ref)

    logits = lax.dot_general(w_ref[...], h_ref[...], (((1,), (1,)), ((), ())), preferred_element_type=F32)
    scores = jax.nn.sigmoid(logits)
    sel = scores + b_ref[...]
    iota_p = lax.broadcasted_iota(jnp.int32, (per, tn), 0)
    gs = []
    for g in range(G):
        blk = sel[g * per:(g + 1) * per]
        m1 = jnp.max(blk, axis=0, keepdims=True)
        i1 = jnp.min(jnp.where(blk == m1, iota_p, per), axis=0, keepdims=True)
        m2 = jnp.max(jnp.where(iota_p == i1, neg, blk), axis=0, keepdims=True)
        gs.append(m1 + m2)
    gsel = jnp.concatenate(gs, axis=0)
    iota_g = lax.broadcasted_iota(jnp.int32, (G, tn), 0)
    gpick = jnp.zeros((G, tn), F32)
    for _ in range(TOPK_GROUPS):
        m = jnp.max(gsel, axis=0, keepdims=True)
        i = jnp.min(jnp.where(gsel == m, iota_g, G), axis=0, keepdims=True)
        hit = iota_g == i
        gpick = jnp.where(hit, 1.0, gpick)
        gsel = jnp.where(hit, neg, gsel)
    emask = jnp.concatenate([jnp.broadcast_to(gpick[g:g + 1], (per, tn)) for g in range(G)], axis=0) > 0.5
    cand = jnp.where(emask, sel, neg)
    iota_e = lax.broadcasted_iota(jnp.int32, (E, tn), 0)
    picked = jnp.zeros((E, tn), F32)
    ids, pick_scores = [], []
    for _ in range(TOP_K):
        m = jnp.max(cand, axis=0, keepdims=True)
        i = jnp.min(jnp.where(cand == m, iota_e, E), axis=0, keepdims=True)
        hit = iota_e == i
        ids.append(i)
        pick_scores.append(jnp.sum(jnp.where(hit, scores, 0.0), axis=0, keepdims=True))
        picked = jnp.where(hit, 1.0, picked)
        cand = jnp.where(hit, neg, cand)
    total = pick_scores[0]
    for sc in pick_scores[1:]:
        total = total + sc
    rr = lax.broadcasted_iota(jnp.int32, (tn, tn), 0)
    cc = lax.broadcasted_iota(jnp.int32, (tn, tn), 1)
    before = (rr < cc).astype(BF16)
    rank = jnp.dot(picked.astype(BF16), before, preferred_element_type=F32) + carry_ref[:, :1]
    ranks = [jnp.sum(jnp.where(iota_e == i, rank, 0.0), axis=0, keepdims=True) for i in ids]
    idx_ref[...] = jnp.concatenate(ids, axis=0)
    rank_ref[...] = jnp.concatenate(ranks, axis=0).astype(jnp.int32)
    wt_ref[...] = jnp.concatenate([sc / (total + 1e-20) * ROUTED_SCALE for sc in pick_scores], axis=0)
    carry_ref[...] = carry_ref[...] + jnp.sum(picked, axis=1, keepdims=True)
    cnt_ref[...] = carry_ref[...]


def route(h, w_router, b_router):
    T, D = h.shape
    tn = _pick_tile(T, ROUTE_TILE)
    pick = pl.BlockSpec((TOP_K, tn), lambda i: (0, i))
    idx, rank, wts, cnt = pl.pallas_call(
        _route_kernel,
        grid=(T // tn,),
        in_specs=[pl.BlockSpec((tn, D), lambda i: (i, 0)),
                  pl.BlockSpec((N_EXPERTS, D), lambda i: (0, 0)),
                  pl.BlockSpec((N_EXPERTS, 1), lambda i: (0, 0))],
        out_specs=[pick, pick, pick, pl.BlockSpec((N_EXPERTS, LANES), lambda i: (0, 0))],
        out_shape=[jax.ShapeDtypeStruct((TOP_K, T), jnp.int32), jax.ShapeDtypeStruct((TOP_K, T), jnp.int32),
                   jax.ShapeDtypeStruct((TOP_K, T), F32), jax.ShapeDtypeStruct((N_EXPERTS, LANES), F32)],
        scratch_shapes=[pltpu.VMEM((N_EXPERTS, LANES), F32)],
        compiler_params=pltpu.CompilerParams(dimension_semantics=("arbitrary",)),
        name="route",
    )(h, w_router.T.astype(BF16), b_router.astype(F32)[:, None])
    return idx, rank, wts, cnt[:, 0].astype(jnp.int32)


ROW_TILE = 256
HALO_ROWS = 16


def _merge_kernel(of_ref, ob_ref, z_ref, u_ref, b_ref, c_ref, up_ref, cp_ref, un_ref, cn_ref, at_ref,
                  g0_ref, g1_ref, g2_ref, h_ref, gate_ref, shift_ref, scale_ref, dnn_ref, wconv_ref, post_ref,
                  pre_ref, wdn_ref, wsc_ref, wat_ref, wout_ref, hn_ref, f_ref):
    i, n_i = pl.program_id(1), pl.num_programs(1)
    tm = h_ref.shape[1]
    o = of_ref[0].astype(F32) + ob_ref[0].astype(F32)
    z = z_ref[0].astype(F32)
    parts = []
    for h in range(DN_HEADS):
        cols = slice(h * DN_DV, (h + 1) * DN_DV)
        zh = z[:, cols]
        parts.append(_rms(o[:, cols]) * dnn_ref[...] * (zh * jax.nn.sigmoid(zh)))
    y_dn = jnp.dot(jnp.concatenate(parts, axis=1).astype(BF16), wdn_ref[...], preferred_element_type=F32)

    cu = c_ref[0].astype(F32) * u_ref[0].astype(F32)
    last = HALO_ROWS - 1
    prev_row = cp_ref[0, last:last + 1, :].astype(F32) * up_ref[0, last:last + 1, :].astype(F32)
    next_row = cn_ref[0, 0:1, :].astype(F32) * un_ref[0, 0:1, :].astype(F32)
    prev_row = jnp.where(i == 0, 0.0, prev_row)
    next_row = jnp.where(i == n_i - 1, 0.0, next_row)
    rows = lax.broadcasted_iota(jnp.int32, (tm, 1), 0)
    cu_prev = jnp.where(rows == 0, prev_row, pltpu.roll(cu, 1, 0))
    cu_next = jnp.where(rows == tm - 1, next_row, pltpu.roll(cu, tm - 1, 0))
    wc = wconv_ref[...]
    conv = cu_prev * wc[0:1] + cu * wc[1:2] + cu_next * wc[2:3]
    y_sc = jnp.dot((b_ref[0].astype(F32) * conv).astype(BF16), wsc_ref[...], preferred_element_type=F32)

    y_at = jnp.dot(at_ref[0], wat_ref[...], preferred_element_type=F32)
    comb = (jax.nn.sigmoid(g0_ref[0].astype(F32)) * y_dn + jax.nn.sigmoid(g1_ref[0].astype(F32)) * y_sc
            + jax.nn.sigmoid(g2_ref[0].astype(F32)) * y_at)
    y = jnp.dot(comb.astype(BF16), wout_ref[...], preferred_element_type=F32)
    hn = h_ref[0] + gate_ref[0] * (_rms(y) * post_ref[...])
    hn_ref[0] = hn
    f_ref[0] = (_rms(hn) * pre_ref[...] * (1.0 + scale_ref[0]) + shift_ref[0]).astype(f_ref.dtype)


def merge(o_f, o_b, o_row0, p, o_attn, h, gate, shift, scale, dn_norm, w_sc_conv, mix_post, ffn_pre,
          w_proj_dn, w_proj_sc, w_proj_attn, w_out):
    B, Ls, D = h.shape
    assert D == ATTN_Q_W and OFF_G % D == 0
    tm = _pick_tile(Ls, ROW_TILE)
    assert o_row0 % tm == 0 and tm % HALO_ROWS == 0
    r0 = o_row0 // tm
    hb = tm // HALO_ROWS
    n_halo = Ls // HALO_ROWS
    W = SC_WIDTH
    col = lambda off, w: pl.BlockSpec((1, tm, w), lambda b, i: (b, i, off // w))
    prev = lambda off: pl.BlockSpec((1, HALO_ROWS, W), lambda b, i: (b, jnp.maximum(i * hb - 1, 0), off // W))
    nxt = lambda off: pl.BlockSpec((1, HALO_ROWS, W), lambda b, i: (b, jnp.minimum((i + 1) * hb, n_halo - 1), off // W))
    o_spec = pl.BlockSpec((1, tm, DN_V_W), lambda b, i: (b, r0 + i, 0))
    mod = pl.BlockSpec((1, 1, D), lambda b, i: (b, 0, 0))
    full = lambda a: pl.BlockSpec(a.shape, lambda b, i: (0,) * a.ndim)
    row = pl.BlockSpec((1, tm, D), lambda b, i: (b, i, 0))
    vecs = [dn_norm.astype(F32)[None], w_sc_conv.astype(F32), mix_post.astype(F32)[None], ffn_pre.astype(F32)[None]]
    ws = [w_proj_dn.astype(BF16), w_proj_sc.astype(BF16), w_proj_attn.astype(BF16), w_out.astype(BF16)]
    return pl.pallas_call(
        _merge_kernel,
        grid=(B, Ls // tm),
        in_specs=[o_spec, o_spec, col(OFF_Z, W), col(OFF_U, W), col(OFF_B, W), col(OFF_C, W),
                  prev(OFF_U), prev(OFF_C), nxt(OFF_U), nxt(OFF_C), row,
                  col(OFF_G, D), col(OFF_G + D, D), col(OFF_G + 2 * D, D), row, mod, mod, mod]
                 + [full(a) for a in vecs] + [full(a) for a in ws],
        out_specs=[row, row],
        out_shape=[jax.ShapeDtypeStruct((B, Ls, D), F32), jax.ShapeDtypeStruct((B, Ls, D), BF16)],
        compiler_params=pltpu.CompilerParams(vmem_limit_bytes=48 * 1024 * 1024),
        name="merge",
    )(o_f, o_b, p, p, p, p, p, p, p, p, o_attn, p, p, p, h, gate, shift, scale, *vecs, *ws)


def _combine_kernel(*refs, with_next):
    yg_ref, wt_ref, f_ref, wsg_ref, wsu_ref, wsd_ref, h_ref, gate_ref, post_ref = refs[:9]
    wt = wt_ref[...]
    acc = yg_ref[0].astype(F32) * wt[:, 0:1]
    for k in range(1, TOP_K):
        acc = acc + yg_ref[k].astype(F32) * wt[:, k:k + 1]
    x = f_ref[...]
    g = jnp.dot(x, wsg_ref[...], preferred_element_type=F32)
    u = jnp.dot(x, wsu_ref[...], preferred_element_type=F32)
    a = (g * jax.nn.sigmoid(g) * u).astype(BF16)
    y = acc + jnp.dot(a, wsd_ref[...], preferred_element_type=F32)
    hn = h_ref[...] + gate_ref[0] * (_rms(y) * post_ref[...])
    if with_next:
        pre_ref, scale_ref, shift_ref, hn_ref, nx_ref = refs[9:]
        nx_ref[...] = (_rms(hn) * pre_ref[...] * (1.0 + scale_ref[0]) + shift_ref[0]).astype(nx_ref.dtype)
    else:
        hn_ref, = refs[9:]
    hn_ref[...] = hn


def combine(yg, wt, f, tok0, ws_gate, ws_up, ws_down, h, gate, ffn_post, nxt=None):
    B, Ls, D = h.shape
    tm = _pick_tile(Ls, ROW_TILE)
    assert tok0 % tm == 0
    t0 = tok0 // tm
    per_b = Ls // tm
    F = ws_gate.shape[-1]
    tok = pl.BlockSpec((tm, D), lambda i: (t0 + i, 0))
    row = pl.BlockSpec((tm, D), lambda i: (i, 0))
    mod = pl.BlockSpec((1, 1, D), lambda i: (i // per_b, 0, 0))
    vec = pl.BlockSpec((1, D), lambda i: (0, 0))
    in_specs = [pl.BlockSpec((TOP_K, tm, D), lambda i: (0, t0 + i, 0)),
                pl.BlockSpec((tm, TOP_K), lambda i: (t0 + i, 0)), tok,
                pl.BlockSpec((D, F), lambda i: (0, 0)), pl.BlockSpec((D, F), lambda i: (0, 0)),
                pl.BlockSpec((F, D), lambda i: (0, 0)), row, mod, vec]
    args = [yg, wt, f, ws_gate.astype(BF16), ws_up.astype(BF16), ws_down.astype(BF16), h.reshape(B * Ls, D), gate,
            ffn_post.astype(F32)[None]]
    out_specs = [row]
    out_shape = [jax.ShapeDtypeStruct((B * Ls, D), F32)]
    if nxt is not None:
        in_specs += [vec, mod, mod]
        args += [nxt[0].astype(F32)[None], nxt[1], nxt[2]]
        out_specs.append(row)
        out_shape.append(jax.ShapeDtypeStruct((B * Ls, D), BF16))
    outs = pl.pallas_call(
        functools.partial(_combine_kernel, with_next=nxt is not None),
        grid=(B * Ls // tm,),
        in_specs=in_specs,
        out_specs=out_specs,
        out_shape=out_shape,
        compiler_params=pltpu.CompilerParams(vmem_limit_bytes=48 * 1024 * 1024),
        name="combine",
    )(*args)
    return (outs[0].reshape(B, Ls, D), outs[1] if nxt is not None else None)


def rmsnorm(x, w):
    xf = x.astype(F32)
    y = xf * lax.rsqrt(jnp.mean(xf * xf, axis=-1, keepdims=True) + NORM_EPS)
    return y * w.astype(F32)


def rope_tables(rows):
    row_pos = jnp.repeat(jnp.arange(rows, dtype=F32), GRID_W)
    col_pos = jnp.tile(jnp.arange(GRID_W, dtype=F32), rows)
    axis_dim = ATTN_HD // 2
    inv_freq = ROPE_THETA ** (-jnp.arange(0, axis_dim, 2, dtype=F32) / axis_dim)
    ang = jnp.concatenate([row_pos[:, None] * inv_freq, col_pos[:, None] * inv_freq], axis=-1)
    cos, sin = jnp.cos(ang), jnp.sin(ang)
    return jnp.concatenate([cos, cos], axis=-1), jnp.concatenate([-sin, sin], axis=-1)


def _deinterleave_perm(n_heads):
    base = np.concatenate([np.arange(0, ATTN_HD, 2), np.arange(1, ATTN_HD, 2)])
    return np.concatenate([h * ATTN_HD + base for h in range(n_heads)])


def prep_w_in(w_in):
    o_a = DN_QK_W + DN_V_W
    o_atk = o_a + 4 * DN_HEADS
    o_atv = o_atk + ATTN_KV_W
    o_rest = o_atv + ATTN_KV_W
    o_atq = o_rest + DN_QK_W + DN_V_W + 3 * SC_WIDTH
    o_gates = o_atq + ATTN_Q_W
    w_kv = w_in[:, :o_a]
    w_ab = w_in[:, o_a:o_atk]
    w_atk = w_in[:, o_atk:o_atv][:, _deinterleave_perm(ATTN_KV_HEADS)]
    w_atv = w_in[:, o_atv:o_rest]
    w_pre_q = w_in[:, o_rest:o_atq]
    w_atq = w_in[:, o_atq:o_gates][:, _deinterleave_perm(ATTN_HEADS)]
    w_gates = w_in[:, o_gates:]
    w_main = jnp.concatenate([w_kv, w_atk, w_atv, w_pre_q, w_atq, w_gates], axis=1).astype(BF16)
    w_ab = jnp.pad(w_ab, ((0, 0), (0, LANES - w_ab.shape[1]))).astype(BF16)
    return w_main, w_ab


KV_MAIN = OFF_Q


def decay_beta(ab, dn_a_log, dn_dt_bias):
    B, L = ab.shape[:2]
    a = ab[..., :2 * DN_HEADS].reshape(B, L, 2, DN_HEADS)
    g = -jnp.exp(dn_a_log.astype(F32)) * jax.nn.softplus(a + dn_dt_bias.astype(F32))
    beta = jax.nn.sigmoid(ab[..., 2 * DN_HEADS:4 * DN_HEADS].reshape(B, L, 2, DN_HEADS))
    return g, beta


def token_mixers(hl, hc, w_in, w_dn_conv_q, w_dn_conv_kv, dn_a_log, dn_dt_bias, q_norm, k_norm, cos2, sin2, ctx_out):
    L = cos2.shape[0]
    B = hl.shape[0] // L
    Lc = hc.shape[0] // B
    w_main, w_ab = prep_w_in(w_in)
    base = _deinterleave_perm(1)
    q_norm_s = (q_norm[base] * ATTN_HD ** -0.5).astype(F32)
    k_norm_p = k_norm[base].astype(F32)

    p_l = matmul(hl, w_main, BF16).reshape(B, L, -1)
    ab_l = matmul(hl, w_ab, F32).reshape(B, L, -1)
    p_c = matmul(hc, w_main if ctx_out else w_main[:, :KV_MAIN], BF16).reshape(B, Lc, -1)
    ab_c = matmul(hc, w_ab, F32).reshape(B, Lc, -1)

    g_c, beta_c = decay_beta(ab_c, dn_a_log, dn_dt_bias)
    g_l, beta_l = decay_beta(ab_l, dn_a_log, dn_dt_bias)
    o_f, o_b = delta_rule(p_c, p_l, w_dn_conv_kv, w_dn_conv_q, jnp.concatenate([g_c, g_l], axis=1),
                          jnp.concatenate([beta_c, beta_l], axis=1), ctx_out)

    o_at_l = attention(p_l, p_c, p_l, q_norm_s, k_norm_p, cos2, sin2)
    o_at_c = attention(p_c, p_c, None, q_norm_s, k_norm_p, None, None) if ctx_out else None
    return p_l, p_c, o_f, o_b, o_at_l, o_at_c


def moe_routed(f, w_router, b_router, w_gate, w_up, w_down):
    T, D = f.shape
    idx, rank, wts, counts = route(f, w_router, b_router)
    n_blocks = -(-(T * TOP_K) // MOE_BLOCK) + N_EXPERTS
    n_slots = n_blocks * MOE_BLOCK
    padded = (counts + MOE_BLOCK - 1) // MOE_BLOCK * MOE_BLOCK
    pad_end = jnp.cumsum(padded)
    pad_start = pad_end - padded
    experts = jnp.arange(N_EXPERTS, dtype=jnp.int32)
    dest = jnp.sum(jnp.where(idx[..., None] == experts, pad_start, 0), axis=-1) + rank
    tok = jnp.broadcast_to(jnp.arange(T, dtype=jnp.int32)[None], (TOP_K, T))
    slot_tok = jnp.full((n_slots,), T, jnp.int32).at[dest.reshape(-1)].set(tok.reshape(-1), unique_indices=True)
    block_start = jnp.arange(n_blocks, dtype=jnp.int32) * MOE_BLOCK
    block_e = jnp.minimum(jnp.sum(pad_end[None, :] <= block_start[:, None], axis=1), N_EXPERTS - 1)
    f_pad = jnp.concatenate([f, jnp.zeros((1, D), BF16)], axis=0)
    y_sorted = expert_blocks(f_pad[slot_tok], block_e, w_gate, w_up, w_down)
    return y_sorted[dest], wts.T


def kernel(x, c, ctx, c_ctx, w_mod, b_mod, mix_pre, mix_post, ffn_pre, ffn_post, w_in, w_dn_conv_q, w_dn_conv_kv, dn_a_log, dn_dt_bias, dn_norm, w_sc_conv, q_norm, k_norm, w_proj_dn, w_proj_sc, w_proj_attn, w_out, w_router, b_router, w_exp_gate, w_exp_up, w_exp_down, w_sh_gate, w_sh_up, w_sh_down):
    B, L, D = x.shape
    Lc = ctx.shape[1]
    depth = w_in.shape[0]
    cos2, sin2 = rope_tables(L // GRID_W)
    silu_all = jax.nn.silu(jnp.concatenate([c, c_ctx[None]], axis=0))
    n_pad = -(-(B + 1) // 8) * 8
    silu_all = jnp.pad(silu_all, ((0, n_pad - (B + 1)), (0, 0)))
    mods = [matmul(silu_all, w_mod[layer], F32) + b_mod[layer] for layer in range(depth)]
    lat_mod = lambda layer, j: mods[layer][:B, None, j * D:(j + 1) * D]
    ctx_mod = lambda layer, j: jnp.broadcast_to(mods[layer][B, j * D:(j + 1) * D], (B, 1, D))

    h_lat, h_ctx = x, ctx
    hl = (rmsnorm(x, mix_pre[0]) * (1 + lat_mod(0, 1)) + lat_mod(0, 0)).astype(BF16).reshape(B * L, D)
    hc = (rmsnorm(ctx, mix_pre[0]) * (1 + ctx_mod(0, 1)) + ctx_mod(0, 0)).astype(BF16).reshape(B * Lc, D)
    for layer in range(depth):
        ctx_out = layer < depth - 1
        p_l, p_c, o_f, o_b, o_at_l, o_at_c = token_mixers(
            hl, hc, w_in[layer], w_dn_conv_q[layer], w_dn_conv_kv[layer], dn_a_log[layer], dn_dt_bias[layer],
            q_norm[layer], k_norm[layer], cos2, sin2, ctx_out)
        merge_w = (dn_norm[layer], w_sc_conv[layer], mix_post[layer], ffn_pre[layer],
                   w_proj_dn[layer], w_proj_sc[layer], w_proj_attn[layer], w_out[layer])
        h_lat, f_l = merge(o_f, o_b, Lc, p_l, o_at_l, h_lat, lat_mod(layer, 2), lat_mod(layer, 3), lat_mod(layer, 4),
                           *merge_w)
        f = f_l.reshape(B * L, D)
        if ctx_out:
            h_ctx, f_c = merge(o_f, o_b, 0, p_c, o_at_c, h_ctx, ctx_mod(layer, 2), ctx_mod(layer, 3),
                               ctx_mod(layer, 4), *merge_w)
            f = jnp.concatenate([f, f_c.reshape(B * Lc, D)], axis=0)
        yg, wt = moe_routed(f, w_router[layer], b_router[layer], w_exp_gate[layer], w_exp_up[layer], w_exp_down[layer])
        shared_w = (w_sh_gate[layer], w_sh_up[layer], w_sh_down[layer])
        nxt_l = nxt_c = None
        if ctx_out:
            nxt_l = (mix_pre[layer + 1], lat_mod(layer + 1, 1), lat_mod(layer + 1, 0))
            nxt_c = (mix_pre[layer + 1], ctx_mod(layer + 1, 1), ctx_mod(layer + 1, 0))
            h_ctx, hc = combine(yg, wt, f, B * L, *shared_w, h_ctx, ctx_mod(layer, 5), ffn_post[layer], nxt_c)
        h_lat, hl = combine(yg, wt, f, 0, *shared_w, h_lat, lat_mod(layer, 5), ffn_post[layer], nxt_l)
    return h_lat
```

```python
import functools

import jax
import jax.numpy as jnp
import numpy as np
from jax import lax
from jax.experimental import pallas as pl
from jax.experimental.pallas import tpu as pltpu

F32 = jnp.float32
BF16 = jnp.bfloat16

GRID_W = 64
NORM_EPS = 1e-6
DN_HEADS = 4
DN_DK = 128
DN_DV = 128
DN_CHUNK = 64
SC_WIDTH = 512
ATTN_HEADS = 8
ATTN_KV_HEADS = 2
ATTN_GROUP = ATTN_HEADS // ATTN_KV_HEADS
ATTN_HD = 128
ROPE_THETA = 10000.0
N_EXPERTS = 64
TOP_K = 8
N_GROUPS = 8
TOPK_GROUPS = 4
ROUTED_SCALE = 2.5
MOE_BLOCK = 256
N_BRANCH = 3

DN_QK_W = DN_HEADS * DN_DK
DN_V_W = DN_HEADS * DN_DV
ATTN_Q_W = ATTN_HEADS * ATTN_HD
ATTN_KV_W = ATTN_KV_HEADS * ATTN_HD
LANES = 128


def _mm_kernel(x_ref, w_ref, o_ref):
    o_ref[...] = jnp.dot(x_ref[...], w_ref[...], preferred_element_type=F32).astype(o_ref.dtype)


def _pick_tile(n, want):
    t = min(n, want)
    while n % t:
        t //= 2
    return t


def matmul(x, w, out_dtype, tm=512, tn=1024):
    M, K = x.shape
    N = w.shape[1]
    tm = _pick_tile(M, tm)
    tn = _pick_tile(N, tn)
    return pl.pallas_call(
        _mm_kernel,
        grid=(N // tn, M // tm),
        in_specs=[pl.BlockSpec((tm, K), lambda j, i: (i, 0)),
                  pl.BlockSpec((K, tn), lambda j, i: (0, j))],
        out_specs=pl.BlockSpec((tm, tn), lambda j, i: (i, j)),
        out_shape=jax.ShapeDtypeStruct((M, N), out_dtype),
        name="matmul",
    )(x.astype(BF16), w.astype(BF16))


OFF_Q = DN_QK_W + DN_V_W + 2 * ATTN_KV_W
OFF_Z = OFF_Q + DN_QK_W
OFF_U = OFF_Z + DN_V_W
OFF_B = OFF_U + SC_WIDTH
OFF_C = OFF_B + SC_WIDTH
OFF_ATQ = OFF_C + SC_WIDTH
OFF_G = OFF_ATQ + ATTN_Q_W


def _rms(x):
    return x * lax.rsqrt(jnp.mean(x * x, axis=-1, keepdims=True) + NORM_EPS)


K_COL0 = (DN_QK_W + DN_V_W) // ATTN_HD
V_COL0 = (DN_QK_W + DN_V_W + ATTN_KV_W) // ATTN_HD


def _head_norm_rope(x, w, cos2, sin2):
    y = _rms(x) * w
    if cos2 is None:
        return y
    return y * cos2 + pltpu.roll(y, ATTN_HD // 2, 1) * sin2


def _attn_kernel(*refs, n_ctx, n_lat, rope_q):
    if n_lat:
        q_ref, qn_ref, kn_ref, cq_ref, sq_ref, kc_ref, vc_ref, kl_ref, vl_ref, ck_ref, sk_ref, o_ref, k_s, v_s = refs
    else:
        q_ref, qn_ref, kn_ref, kc_ref, vc_ref, o_ref, k_s, v_s = refs

    @pl.when(pl.program_id(2) == 0)
    def _():
        k_s[0:n_ctx, :] = _head_norm_rope(kc_ref[0].astype(F32), kn_ref[...], None, None).astype(BF16)
        v_s[0:n_ctx, :] = vc_ref[0]
        if n_lat:
            k_s[n_ctx:n_ctx + n_lat, :] = _head_norm_rope(kl_ref[0].astype(F32), kn_ref[...], ck_ref[...],
                                                          sk_ref[...]).astype(BF16)
            v_s[n_ctx:n_ctx + n_lat, :] = vl_ref[0]

    k, v = k_s[...], v_s[...]
    cos2 = cq_ref[...] if rope_q else None
    sin2 = sq_ref[...] if rope_q else None
    for g in range(ATTN_GROUP):
        cols = slice(g * ATTN_HD, (g + 1) * ATTN_HD)
        q = _head_norm_rope(q_ref[0, :, cols].astype(F32), qn_ref[...], cos2, sin2).astype(BF16)
        s = lax.dot_general(q, k, (((1,), (1,)), ((), ())), preferred_element_type=F32)
        m = jnp.max(s, axis=-1, keepdims=True)
        p = jnp.exp(s - m)
        l = jnp.sum(p, axis=-1, keepdims=True)
        o = jnp.dot(p.astype(BF16), v, preferred_element_type=F32)
        o_ref[0, :, cols] = (o / l).astype(o_ref.dtype)


def attention(p_q, p_c, p_l, q_norm_s, k_norm_p, cos2, sin2, tq=256):
    B, Lq, _ = p_q.shape
    Lc = p_c.shape[1]
    L = 0 if p_l is None else p_l.shape[1]
    tq = _pick_tile(Lq, tq)
    gw = ATTN_GROUP * ATTN_HD
    q_spec = pl.BlockSpec((1, tq, gw), lambda b, h, i: (b, i, OFF_ATQ // gw + h))
    vec = pl.BlockSpec((1, ATTN_HD), lambda b, h, i: (0, 0))
    kv = lambda n, c0: pl.BlockSpec((1, n, ATTN_HD), lambda b, h, i: (b, 0, c0 + h))
    in_specs = [q_spec, vec, vec]
    args = [p_q, q_norm_s[None], k_norm_p[None]]
    if L:
        in_specs += [pl.BlockSpec((tq, ATTN_HD), lambda b, h, i: (i, 0))] * 2
        args += [cos2, sin2]
    in_specs += [kv(Lc, K_COL0), kv(Lc, V_COL0)]
    args += [p_c, p_c]
    if L:
        in_specs += [kv(L, K_COL0), kv(L, V_COL0), pl.BlockSpec((L, ATTN_HD), lambda b, h, i: (0, 0)),
                     pl.BlockSpec((L, ATTN_HD), lambda b, h, i: (0, 0))]
        args += [p_l, p_l, cos2, sin2]
    return pl.pallas_call(
        functools.partial(_attn_kernel, n_ctx=Lc, n_lat=L, rope_q=bool(L)),
        grid=(B, ATTN_KV_HEADS, Lq // tq),
        in_specs=in_specs,
        out_specs=pl.BlockSpec((1, tq, gw), lambda b, h, i: (b, i, h)),
        out_shape=jax.ShapeDtypeStruct((B, Lq, ATTN_Q_W), BF16),
        scratch_shapes=[pltpu.VMEM((Lc + L, ATTN_HD), BF16), pltpu.VMEM((Lc + L, ATTN_HD), BF16)],
        compiler_params=pltpu.CompilerParams(dimension_semantics=("arbitrary", "arbitrary", "arbitrary")),
        name="attention",
    )(*args)


def _expert_kernel(be_ref, x_ref, wg_ref, wu_ref, wd_ref, o_ref, wg_s, wu_s, wd_s):
    i = pl.program_id(0)

    @pl.when((i == 0) | (be_ref[i] != be_ref[jnp.maximum(i - 1, 0)]))
    def _():
        wg_s[...] = wg_ref[0].astype(BF16)
        wu_s[...] = wu_ref[0].astype(BF16)
        wd_s[...] = wd_ref[0].astype(BF16)

    x = x_ref[...]
    g = jnp.dot(x, wg_s[...], preferred_element_type=F32)
    u = jnp.dot(x, wu_s[...], preferred_element_type=F32)
    a = (g * jax.nn.sigmoid(g) * u).astype(BF16)
    o_ref[...] = jnp.dot(a, wd_s[...], preferred_element_type=F32).astype(o_ref.dtype)


def expert_blocks(x_sorted, block_e, wg, wu, wd, tm=MOE_BLOCK):
    n_rows, D = x_sorted.shape
    F = wg.shape[-1]
    n_blocks = n_rows // tm
    grid_spec = pltpu.PrefetchScalarGridSpec(
        num_scalar_prefetch=1,
        grid=(n_blocks,),
        in_specs=[pl.BlockSpec((tm, D), lambda i, be: (i, 0)),
                  pl.BlockSpec((1, D, F), lambda i, be: (be[i], 0, 0)),
                  pl.BlockSpec((1, D, F), lambda i, be: (be[i], 0, 0)),
                  pl.BlockSpec((1, F, D), lambda i, be: (be[i], 0, 0))],
        out_specs=pl.BlockSpec((tm, D), lambda i, be: (i, 0)),
        scratch_shapes=[pltpu.VMEM((D, F), BF16), pltpu.VMEM((D, F), BF16), pltpu.VMEM((F, D), BF16)],
    )
    return pl.pallas_call(
        _expert_kernel,
        grid_spec=grid_spec,
        out_shape=jax.ShapeDtypeStruct((n_rows, D), BF16),
        compiler_params=pltpu.CompilerParams(dimension_semantics=("arbitrary",)),
        name="expert_blocks",
    )(block_e.astype(jnp.int32), x_sorted, wg, wu, wd)


DN_STACK = DN_HEADS * DN_CHUNK
DN_UNROLL = 2
DN_SPLIT_STEPS = 3
CONV_HALO = 16


def _conv_silu(x_ref, w_ref, r0, n_rows):
    C = DN_CHUNK
    x = x_ref[0, pl.ds(r0, C), :].astype(F32)
    lo = pl.multiple_of(jnp.maximum(r0 - CONV_HALO, 0), CONV_HALO)
    hi = pl.multiple_of(jnp.minimum(r0 + C, n_rows - CONV_HALO), CONV_HALO)
    before = x_ref[0, pl.ds(lo, CONV_HALO), :][CONV_HALO - 1:CONV_HALO].astype(F32)
    after = x_ref[0, pl.ds(hi, CONV_HALO), :][0:1].astype(F32)
    before = jnp.where(r0 > 0, before, 0.0)
    after = jnp.where(r0 + C < n_rows, after, 0.0)
    rows = lax.broadcasted_iota(jnp.int32, (C, 1), 0)
    x_prev = jnp.where(rows == 0, before, pltpu.roll(x, 1, 0))
    x_next = jnp.where(rows == C - 1, after, pltpu.roll(x, C - 1, 0))
    w = w_ref[...]
    y = x_prev * w[0:1] + x * w[1:2] + x_next * w[2:3]
    return y * jax.nn.sigmoid(y)


def _stack_heads(x, heads, l2_scale=None):
    parts = []
    for h in heads:
        xh = x[:, h * DN_DK:(h + 1) * DN_DK]
        if l2_scale is not None:
            xh = xh * (lax.rsqrt(jnp.sum(xh * xh, axis=-1, keepdims=True) + NORM_EPS) * l2_scale)
        parts.append(xh)
    return jnp.concatenate(parts, axis=0)


def _delta_kernel(*refs, n_ctx, n_lat, ctx_q):
    if ctx_q:
        (kc_ref, vc_ref, qc_ref, kl_ref, vl_ref, ql_ref, wk_ref, wv_ref, wq_ref,
         gcol_ref, glast_ref, beta_ref, grow_ref, of_ref, ob_ref, s_ref) = refs
    else:
        (kc_ref, vc_ref, kl_ref, vl_ref, ql_ref, wk_ref, wv_ref, wq_ref,
         gcol_ref, glast_ref, beta_ref, grow_ref, of_ref, ob_ref, s_ref) = refs
        qc_ref = None
    C, R, H = DN_CHUNK, DN_STACK, DN_HEADS
    s_ref[...] = jnp.zeros_like(s_ref)
    row = lax.broadcasted_iota(jnp.int32, (R, R), 0)
    col = lax.broadcasted_iota(jnp.int32, (R, R), 1)
    same_head = (row // C) == (col // C)
    strict = (same_head & (col < row), same_head & (col > row))
    eye = (row == col).astype(F32)

    def make_body(k_ref, v_ref, q_ref, n_rows, row0, unroll):
        n_seq = n_rows // C
        heads = range(H)

        def prepare(d, c):
            r_in = pl.multiple_of(c * C, C)
            r0 = pl.multiple_of(row0 + c * C, C)
            ks = _stack_heads(_conv_silu(k_ref, wk_ref, r_in, n_rows), heads, 1.0)
            vs = _stack_heads(_conv_silu(v_ref, wv_ref, r_in, n_rows), heads)
            if q_ref is None:
                qs = jnp.zeros((R, DN_DK), F32)
            else:
                qs = _stack_heads(_conv_silu(q_ref, wq_ref, r_in, n_rows), heads, DN_DK ** -0.5)

            def col_stack(ref):
                blk = ref[0, pl.ds(r0, C), :]
                return jnp.concatenate([blk[:, d * H + h:d * H + h + 1] for h in heads], axis=0)

            gc, gl, bt = col_stack(gcol_ref), col_stack(glast_ref), col_stack(beta_ref)
            gr = grow_ref[0, d, pl.ds(row0 // C + c, 1), :]
            ks_b = ks.astype(BF16)
            kb = ks * bt
            decay = jnp.exp(jnp.where(strict[d], gc - gr, -jnp.inf))
            g_kk = lax.dot_general(kb.astype(BF16), ks_b, (((1,), (1,)), ((), ())), preferred_element_type=F32)
            g_qk = lax.dot_general(qs.astype(BF16), ks_b, (((1,), (1,)), ((), ())), preferred_element_type=F32)
            e_gc = jnp.exp(gc)
            return dict(d=d, r0=r0, gl=gl,
                        p=(-(g_kk * decay)).astype(BF16),
                        qk=(g_qk * (decay + eye)).astype(BF16),
                        x=jnp.concatenate([vs * bt, kb * e_gc], axis=1),
                        kdec=(ks * jnp.exp(gl - gc)).astype(BF16),
                        qg=(qs * e_gc).astype(BF16))

        def body(n, carry):
            chains = []
            for u in range(unroll):
                m = n * unroll + u
                for d in range(2):
                    chains.append(prepare(d, m if d == 0 else n_seq - 1 - m))
            for it in range(6):
                for ch in chains:
                    x = ch["x"]
                    x_hi = x.astype(BF16)
                    step = jnp.dot(ch["p"], x_hi, preferred_element_type=F32)
                    if it < DN_SPLIT_STEPS:
                        x_lo = (x - x_hi.astype(F32)).astype(BF16)
                        step = step + jnp.dot(ch["p"], x_lo, preferred_element_type=F32)
                    ch["x"] = x + step
                if it < 5:
                    for ch in chains:
                        ch["p"] = jnp.dot(ch["p"], ch["p"], preferred_element_type=F32).astype(BF16)
            for u in range(unroll):
                pair = chains[2 * u:2 * u + 2]
                us, os_ = [[] for _ in pair], [[] for _ in pair]
                for h in heads:
                    sl = slice(h * C, (h + 1) * C)
                    for ci, ch in enumerate(pair):
                        d, gl = ch["d"], ch["gl"]
                        s_old = s_ref[d, h]
                        s_b = s_old.astype(BF16)
                        u_h = ch["x"][sl, :DN_DV] - jnp.dot(ch["x"][sl, DN_DV:].astype(BF16), s_b,
                                                            preferred_element_type=F32)
                        os_[ci].append(jnp.dot(ch["qg"][sl], s_b, preferred_element_type=F32))
                        cd = jnp.exp(gl[h * C:h * C + 1, :])
                        s_ref[d, h] = s_old * cd + lax.dot_general(ch["kdec"][sl], u_h.astype(BF16),
                                                                   (((0,), (0,)), ((), ())), preferred_element_type=F32)
                        us[ci].append(u_h)
                for ci, ch in enumerate(pair):
                    o = jnp.concatenate(os_[ci], axis=0) + jnp.dot(ch["qk"], jnp.concatenate(us[ci], axis=0).astype(BF16),
                                                                   preferred_element_type=F32)
                    o_ref = of_ref if ch["d"] == 0 else ob_ref
                    for h in heads:
                        o_ref[0, pl.ds(ch["r0"], C), h * DN_DV:(h + 1) * DN_DV] = o[h * C:(h + 1) * C].astype(o_ref.dtype)
            return carry

        return body, n_seq // unroll

    for k_ref, v_ref, q_ref, n_rows, row0 in ((kc_ref, vc_ref, qc_ref, n_ctx, 0), (kl_ref, vl_ref, ql_ref, n_lat, n_ctx)):
        unroll = DN_UNROLL if (n_rows // C) % DN_UNROLL == 0 else 1
        body, trips = make_body(k_ref, v_ref, q_ref, n_rows, row0, unroll)
        lax.fori_loop(0, trips, body, 0)


def delta_rule(p_c, p_l, w_conv_kv, w_conv_q, g, beta, ctx_q):
    B, Lc = p_c.shape[:2]
    L = p_l.shape[1]
    Lt = Lc + L
    H, C, W = DN_HEADS, DN_CHUNK, DN_QK_W
    nc = Lt // C
    gch = g.reshape(B, nc, C, 2, H)
    g_f = jnp.cumsum(gch[:, :, :, 0], axis=2)
    g_b = jnp.flip(jnp.cumsum(jnp.flip(gch[:, :, :, 1], axis=2), axis=2), axis=2)
    gcum = jnp.stack([g_f, g_b], axis=3)
    gtot = jnp.broadcast_to(jnp.stack([g_f[:, :, -1:], g_b[:, :, :1]], axis=3), gcum.shape)
    gcol = gcum.reshape(B, Lt, 2 * H)
    glast = gtot.reshape(B, Lt, 2 * H)
    grow = jnp.transpose(gcum, (0, 3, 1, 4, 2)).reshape(B, 2, nc, H * C)
    bt = beta.reshape(B, Lt, 2 * H)
    colblk = lambda n, j: pl.BlockSpec((1, n, W), lambda b: (b, 0, j))
    seq = lambda w: pl.BlockSpec((1, Lt, w), lambda b: (b, 0, 0))
    full = lambda a: pl.BlockSpec(a.shape, lambda b: (0,) * a.ndim)
    wk, wv, wq = (w_conv_kv[:, :W].astype(F32), w_conv_kv[:, W:].astype(F32), w_conv_q.astype(F32))
    q_col = OFF_Q // W
    in_specs = [colblk(Lc, 0), colblk(Lc, 1)] + ([colblk(Lc, q_col)] if ctx_q else [])
    args = [p_c, p_c] + ([p_c] if ctx_q else [])
    in_specs += [colblk(L, 0), colblk(L, 1), colblk(L, q_col), full(wk), full(wv), full(wq),
                 seq(2 * H), seq(2 * H), seq(2 * H), pl.BlockSpec((1, 2, nc, H * C), lambda b: (b, 0, 0, 0))]
    args += [p_l, p_l, p_l, wk, wv, wq, gcol, glast, bt, grow]
    return pl.pallas_call(
        functools.partial(_delta_kernel, n_ctx=Lc, n_lat=L, ctx_q=ctx_q),
        grid=(B,),
        in_specs=in_specs,
        out_specs=[seq(W), seq(W)],
        out_shape=[jax.ShapeDtypeStruct((B, Lt, W), BF16)] * 2,
        scratch_shapes=[pltpu.VMEM((2, H, DN_DK, DN_DV), F32)],
        compiler_params=pltpu.CompilerParams(vmem_limit_bytes=48 * 1024 * 1024),
        name="delta_rule",
    )(*args)


ROUTE_TILE = 512


def _route_kernel(h_ref, w_ref, b_ref, idx_ref, rank_ref, wt_ref, cnt_ref, carry_ref):
    E, G, tn = N_EXPERTS, N_GROUPS, h_ref.shape[0]
    per = E // G
    neg = -jnp.inf

    @pl.when(pl.program_id(0) == 0)
    def _():
        carry_ref[...] = jnp.zeros_like(carry_ref)

    logits = lax.dot_general(w_ref[...], h_ref[...], (((1,), (1,)), ((), ())), preferred_element_type=F32)
    scores = jax.nn.sigmoid(logits)
    sel = scores + b_ref[...]
    iota_p = lax.broadcasted_iota(jnp.int32, (per, tn), 0)
    gs = []
    for g in range(G):
        blk = sel[g * per:(g + 1) * per]
        m1 = jnp.max(blk, axis=0, keepdims=True)
        i1 = jnp.min(jnp.where(blk == m1, iota_p, per), axis=0, keepdims=True)
        m2 = jnp.max(jnp.where(iota_p == i1, neg, blk), axis=0, keepdims=True)
        gs.append(m1 + m2)
    gsel = jnp.concatenate(gs, axis=0)
    iota_g = lax.broadcasted_iota(jnp.int32, (G, tn), 0)
    gpick = jnp.zeros((G, tn), F32)
    for _ in range(TOPK_GROUPS):
        m = jnp.max(gsel, axis=0, keepdims=True)
        i = jnp.min(jnp.where(gsel == m, iota_g, G), axis=0, keepdims=True)
        hit = iota_g == i
        gpick = jnp.where(hit, 1.0, gpick)
        gsel = jnp.where(hit, neg, gsel)
    emask = jnp.concatenate([jnp.broadcast_to(gpick[g:g + 1], (per, tn)) for g in range(G)], axis=0) > 0.5
    cand = jnp.where(emask, sel, neg)
    iota_e = lax.broadcasted_iota(jnp.int32, (E, tn), 0)
    picked = jnp.zeros((E, tn), F32)
    ids, pick_scores = [], []
    for _ in range(TOP_K):
        m = jnp.max(cand, axis=0, keepdims=True)
        i = jnp.min(jnp.where(cand == m, iota_e, E), axis=0, keepdims=True)
        hit = iota_e == i
        ids.append(i)
        pick_scores.append(jnp.sum(jnp.where(hit, scores, 0.0), axis=0, keepdims=True))
        picked = jnp.where(hit, 1.0, picked)
        cand = jnp.where(hit, neg, cand)
    total = pick_scores[0]
    for sc in pick_scores[1:]:
        total = total + sc
    rr = lax.broadcasted_iota(jnp.int32, (tn, tn), 0)
    cc = lax.broadcasted_iota(jnp.int32, (tn, tn), 1)
    before = (rr < cc).astype(BF16)
    rank = jnp.dot(picked.astype(BF16), before, preferred_element_type=F32) + carry_ref[:, :1]
    ranks = [jnp.sum(jnp.where(iota_e == i, rank, 0.0), axis=0, keepdims=True) for i in ids]
    idx_ref[...] = jnp.concatenate(ids, axis=0)
    rank_ref[...] = jnp.concatenate(ranks, axis=0).astype(jnp.int32)
    wt_ref[...] = jnp.concatenate([sc / (total + 1e-20) * ROUTED_SCALE for sc in pick_scores], axis=0)
    carry_ref[...] = carry_ref[...] + jnp.sum(picked, axis=1, keepdims=True)
    cnt_ref[...] = carry_ref[...]


def route(h, w_router, b_router):
    T, D = h.shape
    tn = _pick_tile(T, ROUTE_TILE)
    pick = pl.BlockSpec((TOP_K, tn), lambda i: (0, i))
    idx, rank, wts, cnt = pl.pallas_call(
        _route_kernel,
        grid=(T // tn,),
        in_specs=[pl.BlockSpec((tn, D), lambda i: (i, 0)),
                  pl.BlockSpec((N_EXPERTS, D), lambda i: (0, 0)),
                  pl.BlockSpec((N_EXPERTS, 1), lambda i: (0, 0))],
        out_specs=[pick, pick, pick, pl.BlockSpec((N_EXPERTS, LANES), lambda i: (0, 0))],
        out_shape=[jax.ShapeDtypeStruct((TOP_K, T), jnp.int32), jax.ShapeDtypeStruct((TOP_K, T), jnp.int32),
                   jax.ShapeDtypeStruct((TOP_K, T), F32), jax.ShapeDtypeStruct((N_EXPERTS, LANES), F32)],
        scratch_shapes=[pltpu.VMEM((N_EXPERTS, LANES), F32)],
        compiler_params=pltpu.CompilerParams(dimension_semantics=("arbitrary",)),
        name="route",
    )(h, w_router.T.astype(BF16), b_router.astype(F32)[:, None])
    return idx, rank, wts, cnt[:, 0].astype(jnp.int32)


ROW_TILE = 256
HALO_ROWS = 16


def _merge_kernel(of_ref, ob_ref, z_ref, u_ref, b_ref, c_ref, up_ref, cp_ref, un_ref, cn_ref, at_ref,
                  g0_ref, g1_ref, g2_ref, h_ref, gate_ref, shift_ref, scale_ref, dnn_ref, wconv_ref, post_ref,
                  pre_ref, wdn_ref, wsc_ref, wat_ref, wout_ref, hn_ref, f_ref):
    i, n_i = pl.program_id(1), pl.num_programs(1)
    tm = h_ref.shape[1]
    o = of_ref[0].astype(F32) + ob_ref[0].astype(F32)
    z = z_ref[0].astype(F32)
    parts = []
    for h in range(DN_HEADS):
        cols = slice(h * DN_DV, (h + 1) * DN_DV)
        zh = z[:, cols]
        parts.append(_rms(o[:, cols]) * dnn_ref[...] * (zh * jax.nn.sigmoid(zh)))
    y_dn = jnp.dot(jnp.concatenate(parts, axis=1).astype(BF16), wdn_ref[...], preferred_element_type=F32)

    cu = c_ref[0].astype(F32) * u_ref[0].astype(F32)
    last = HALO_ROWS - 1
    prev_row = cp_ref[0, last:last + 1, :].astype(F32) * up_ref[0, last:last + 1, :].astype(F32)
    next_row = cn_ref[0, 0:1, :].astype(F32) * un_ref[0, 0:1, :].astype(F32)
    prev_row = jnp.where(i == 0, 0.0, prev_row)
    next_row = jnp.where(i == n_i - 1, 0.0, next_row)
    rows = lax.broadcasted_iota(jnp.int32, (tm, 1), 0)
    cu_prev = jnp.where(rows == 0, prev_row, pltpu.roll(cu, 1, 0))
    cu_next = jnp.where(rows == tm - 1, next_row, pltpu.roll(cu, tm - 1, 0))
    wc = wconv_ref[...]
    conv = cu_prev * wc[0:1] + cu * wc[1:2] + cu_next * wc[2:3]
    y_sc = jnp.dot((b_ref[0].astype(F32) * conv).astype(BF16), wsc_ref[...], preferred_element_type=F32)

    y_at = jnp.dot(at_ref[0], wat_ref[...], preferred_element_type=F32)
    comb = (jax.nn.sigmoid(g0_ref[0].astype(F32)) * y_dn + jax.nn.sigmoid(g1_ref[0].astype(F32)) * y_sc
            + jax.nn.sigmoid(g2_ref[0].astype(F32)) * y_at)
    y = jnp.dot(comb.astype(BF16), wout_ref[...], preferred_element_type=F32)
    hn = h_ref[0] + gate_ref[0] * (_rms(y) * post_ref[...])
    hn_ref[0] = hn
    f_ref[0] = (_rms(hn) * pre_ref[...] * (1.0 + scale_ref[0]) + shift_ref[0]).astype(f_ref.dtype)


def merge(o_f, o_b, o_row0, p, o_attn, h, gate, shift, scale, dn_norm, w_sc_conv, mix_post, ffn_pre,
          w_proj_dn, w_proj_sc, w_proj_attn, w_out):
    B, Ls, D = h.shape
    assert D == ATTN_Q_W and OFF_G % D == 0
    tm = _pick_tile(Ls, ROW_TILE)
    assert o_row0 % tm == 0 and tm % HALO_ROWS == 0
    r0 = o_row0 // tm
    hb = tm // HALO_ROWS
    n_halo = Ls // HALO_ROWS
    W = SC_WIDTH
    col = lambda off, w: pl.BlockSpec((1, tm, w), lambda b, i: (b, i, off // w))
    prev = lambda off: pl.BlockSpec((1, HALO_ROWS, W), lambda b, i: (b, jnp.maximum(i * hb - 1, 0), off // W))
    nxt = lambda off: pl.BlockSpec((1, HALO_ROWS, W), lambda b, i: (b, jnp.minimum((i + 1) * hb, n_halo - 1), off // W))
    o_spec = pl.BlockSpec((1, tm, DN_V_W), lambda b, i: (b, r0 + i, 0))
    mod = pl.BlockSpec((1, 1, D), lambda b, i: (b, 0, 0))
    full = lambda a: pl.BlockSpec(a.shape, lambda b, i: (0,) * a.ndim)
    row = pl.BlockSpec((1, tm, D), lambda b, i: (b, i, 0))
    vecs = [dn_norm.astype(F32)[None], w_sc_conv.astype(F32), mix_post.astype(F32)[None], ffn_pre.astype(F32)[None]]
    ws = [w_proj_dn.astype(BF16), w_proj_sc.astype(BF16), w_proj_attn.astype(BF16), w_out.astype(BF16)]
    return pl.pallas_call(
        _merge_kernel,
        grid=(B, Ls // tm),
        in_specs=[o_spec, o_spec, col(OFF_Z, W), col(OFF_U, W), col(OFF_B, W), col(OFF_C, W),
                  prev(OFF_U), prev(OFF_C), nxt(OFF_U), nxt(OFF_C), row,
                  col(OFF_G, D), col(OFF_G + D, D), col(OFF_G + 2 * D, D), row, mod, mod, mod]
                 + [full(a) for a in vecs] + [full(a) for a in ws],
        out_specs=[row, row],
        out_shape=[jax.ShapeDtypeStruct((B, Ls, D), F32), jax.ShapeDtypeStruct((B, Ls, D), BF16)],
        compiler_params=pltpu.CompilerParams(vmem_limit_bytes=48 * 1024 * 1024),
        name="merge",
    )(o_f, o_b, p, p, p, p, p, p, p, p, o_attn, p, p, p, h, gate, shift, scale, *vecs, *ws)


def _combine_kernel(*refs, with_next):
    yg_ref, wt_ref, f_ref, wsg_ref, wsu_ref, wsd_ref, h_ref, gate_ref, post_ref = refs[:9]
    wt = wt_ref[...]
    acc = yg_ref[0].astype(F32) * wt[:, 0:1]
    for k in range(1, TOP_K):
        acc = acc + yg_ref[k].astype(F32) * wt[:, k:k + 1]
    x = f_ref[...]
    g = jnp.dot(x, wsg_ref[...], preferred_element_type=F32)
    u = jnp.dot(x, wsu_ref[...], preferred_element_type=F32)
    a = (g * jax.nn.sigmoid(g) * u).astype(BF16)
    y = acc + jnp.dot(a, wsd_ref[...], preferred_element_type=F32)
    hn = h_ref[...] + gate_ref[0] * (_rms(y) * post_ref[...])
    if with_next:
        pre_ref, scale_ref, shift_ref, hn_ref, nx_ref = refs[9:]
        nx_ref[...] = (_rms(hn) * pre_ref[...] * (1.0 + scale_ref[0]) + shift_ref[0]).astype(nx_ref.dtype)
    else:
        hn_ref, = refs[9:]
    hn_ref[...] = hn


def combine(yg, wt, f, tok0, ws_gate, ws_up, ws_down, h, gate, ffn_post, nxt=None):
    B, Ls, D = h.shape
    tm = _pick_tile(Ls, ROW_TILE)
    assert tok0 % tm == 0
    t0 = tok0 // tm
    per_b = Ls // tm
    F = ws_gate.shape[-1]
    tok = pl.BlockSpec((tm, D), lambda i: (t0 + i, 0))
    row = pl.BlockSpec((tm, D), lambda i: (i, 0))
    mod = pl.BlockSpec((1, 1, D), lambda i: (i // per_b, 0, 0))
    vec = pl.BlockSpec((1, D), lambda i: (0, 0))
    in_specs = [pl.BlockSpec((TOP_K, tm, D), lambda i: (0, t0 + i, 0)),
                pl.BlockSpec((tm, TOP_K), lambda i: (t0 + i, 0)), tok,
                pl.BlockSpec((D, F), lambda i: (0, 0)), pl.BlockSpec((D, F), lambda i: (0, 0)),
                pl.BlockSpec((F, D), lambda i: (0, 0)), row, mod, vec]
    args = [yg, wt, f, ws_gate.astype(BF16), ws_up.astype(BF16), ws_down.astype(BF16), h.reshape(B * Ls, D), gate,
            ffn_post.astype(F32)[None]]
    out_specs = [row]
    out_shape = [jax.ShapeDtypeStruct((B * Ls, D), F32)]
    if nxt is not None:
        in_specs += [vec, mod, mod]
        args += [nxt[0].astype(F32)[None], nxt[1], nxt[2]]
        out_specs.append(row)
        out_shape.append(jax.ShapeDtypeStruct((B * Ls, D), BF16))
    outs = pl.pallas_call(
        functools.partial(_combine_kernel, with_next=nxt is not None),
        grid=(B * Ls // tm,),
        in_specs=in_specs,
        out_specs=out_specs,
        out_shape=out_shape,
        compiler_params=pltpu.CompilerParams(vmem_limit_bytes=48 * 1024 * 1024),
        name="combine",
    )(*args)
    return (outs[0].reshape(B, Ls, D), outs[1] if nxt is not None else None)


def rmsnorm(x, w):
    xf = x.astype(F32)
    y = xf * lax.rsqrt(jnp.mean(xf * xf, axis=-1, keepdims=True) + NORM_EPS)
    return y * w.astype(F32)


def rope_tables(rows):
    row_pos = jnp.repeat(jnp.arange(rows, dtype=F32), GRID_W)
    col_pos = jnp.tile(jnp.arange(GRID_W, dtype=F32), rows)
    axis_dim = ATTN_HD // 2
    inv_freq = ROPE_THETA ** (-jnp.arange(0, axis_dim, 2, dtype=F32) / axis_dim)
    ang = jnp.concatenate([row_pos[:, None] * inv_freq, col_pos[:, None] * inv_freq], axis=-1)
    cos, sin = jnp.cos(ang), jnp.sin(ang)
    return jnp.concatenate([cos, cos], axis=-1), jnp.concatenate([-sin, sin], axis=-1)


def _deinterleave_perm(n_heads):
    base = np.concatenate([np.arange(0, ATTN_HD, 2), np.arange(1, ATTN_HD, 2)])
    return np.concatenate([h * ATTN_HD + base for h in range(n_heads)])


def prep_w_in(w_in):
    o_a = DN_QK_W + DN_V_W
    o_atk = o_a + 4 * DN_HEADS
    o_atv = o_atk + ATTN_KV_W
    o_rest = o_atv + ATTN_KV_W
    o_atq = o_rest + DN_QK_W + DN_V_W + 3 * SC_WIDTH
    o_gates = o_atq + ATTN_Q_W
    w_kv = w_in[:, :o_a]
    w_ab = w_in[:, o_a:o_atk]
    w_atk = w_in[:, o_atk:o_atv][:, _deinterleave_perm(ATTN_KV_HEADS)]
    w_atv = w_in[:, o_atv:o_rest]
    w_pre_q = w_in[:, o_rest:o_atq]
    w_atq = w_in[:, o_atq:o_gates][:, _deinterleave_perm(ATTN_HEADS)]
    w_gates = w_in[:, o_gates:]
    w_main = jnp.concatenate([w_kv, w_atk, w_atv, w_pre_q, w_atq, w_gates], axis=1).astype(BF16)
    w_ab = jnp.pad(w_ab, ((0, 0), (0, LANES - w_ab.shape[1]))).astype(BF16)
    return w_main, w_ab


KV_MAIN = OFF_Q


def decay_beta(ab, dn_a_log, dn_dt_bias):
    B, L = ab.shape[:2]
    a = ab[..., :2 * DN_HEADS].reshape(B, L, 2, DN_HEADS)
    g = -jnp.exp(dn_a_log.astype(F32)) * jax.nn.softplus(a + dn_dt_bias.astype(F32))
    beta = jax.nn.sigmoid(ab[..., 2 * DN_HEADS:4 * DN_HEADS].reshape(B, L, 2, DN_HEADS))
    return g, beta


def token_mixers(hl, hc, w_in, w_dn_conv_q, w_dn_conv_kv, dn_a_log, dn_dt_bias, q_norm, k_norm, cos2, sin2, ctx_out):
    L = cos2.shape[0]
    B = hl.shape[0] // L
    Lc = hc.shape[0] // B
    w_main, w_ab = prep_w_in(w_in)
    base = _deinterleave_perm(1)
    q_norm_s = (q_norm[base] * ATTN_HD ** -0.5).astype(F32)
    k_norm_p = k_norm[base].astype(F32)

    p_l = matmul(hl, w_main, BF16).reshape(B, L, -1)
    ab_l = matmul(hl, w_ab, F32).reshape(B, L, -1)
    p_c = matmul(hc, w_main if ctx_out else w_main[:, :KV_MAIN], BF16).reshape(B, Lc, -1)
    ab_c = matmul(hc, w_ab, F32).reshape(B, Lc, -1)

    g_c, beta_c = decay_beta(ab_c, dn_a_log, dn_dt_bias)
    g_l, beta_l = decay_beta(ab_l, dn_a_log, dn_dt_bias)
    o_f, o_b = delta_rule(p_c, p_l, w_dn_conv_kv, w_dn_conv_q, jnp.concatenate([g_c, g_l], axis=1),
                          jnp.concatenate([beta_c, beta_l], axis=1), ctx_out)

    o_at_l = attention(p_l, p_c, p_l, q_norm_s, k_norm_p, cos2, sin2)
    o_at_c = attention(p_c, p_c, None, q_norm_s, k_norm_p, None, None) if ctx_out else None
    return p_l, p_c, o_f, o_b, o_at_l, o_at_c


def moe_routed(f, w_router, b_router, w_gate, w_up, w_down):
    T, D = f.shape
    idx, rank, wts, counts = route(f, w_router, b_router)
    n_blocks = -(-(T * TOP_K) // MOE_BLOCK) + N_EXPERTS
    n_slots = n_blocks * MOE_BLOCK
    padded = (counts + MOE_BLOCK - 1) // MOE_BLOCK * MOE_BLOCK
    pad_end = jnp.cumsum(padded)
    pad_start = pad_end - padded
    experts = jnp.arange(N_EXPERTS, dtype=jnp.int32)
    dest = jnp.sum(jnp.where(idx[..., None] == experts, pad_start, 0), axis=-1) + rank
    tok = jnp.broadcast_to(jnp.arange(T, dtype=jnp.int32)[None], (TOP_K, T))
    slot_tok = jnp.full((n_slots,), T, jnp.int32).at[dest.reshape(-1)].set(tok.reshape(-1), unique_indices=True)
    block_start = jnp.arange(n_blocks, dtype=jnp.int32) * MOE_BLOCK
    block_e = jnp.minimum(jnp.sum(pad_end[None, :] <= block_start[:, None], axis=1), N_EXPERTS - 1)
    f_pad = jnp.concatenate([f, jnp.zeros((1, D), BF16)], axis=0)
    y_sorted = expert_blocks(f_pad[slot_tok], block_e, w_gate, w_up, w_down)
    return y_sorted[dest], wts.T


def forward(x, c, ctx, c_ctx, w_mod, b_mod, mix_pre, mix_post, ffn_pre, ffn_post, w_in, w_dn_conv_q, w_dn_conv_kv, dn_a_log, dn_dt_bias, dn_norm, w_sc_conv, q_norm, k_norm, w_proj_dn, w_proj_sc, w_proj_attn, w_out, w_router, b_router, w_exp_gate, w_exp_up, w_exp_down, w_sh_gate, w_sh_up, w_sh_down):
    B, L, D = x.shape
    Lc = ctx.shape[1]
    depth = w_in.shape[0]
    cos2, sin2 = rope_tables(L // GRID_W)
    silu_all = jax.nn.silu(jnp.concatenate([c, c_ctx[None]], axis=0))
    n_pad = -(-(B + 1) // 8) * 8
    silu_all = jnp.pad(silu_all, ((0, n_pad - (B + 1)), (0, 0)))
    mods = [matmul(silu_all, w_mod[layer], F32) + b_mod[layer] for layer in range(depth)]
    lat_mod = lambda layer, j: mods[layer][:B, None, j * D:(j + 1) * D]
    ctx_mod = lambda layer, j: jnp.broadcast_to(mods[layer][B, j * D:(j + 1) * D], (B, 1, D))

    h_lat, h_ctx = x, ctx
    hl = (rmsnorm(x, mix_pre[0]) * (1 + lat_mod(0, 1)) + lat_mod(0, 0)).astype(BF16).reshape(B * L, D)
    hc = (rmsnorm(ctx, mix_pre[0]) * (1 + ctx_mod(0, 1)) + ctx_mod(0, 0)).astype(BF16).reshape(B * Lc, D)
    for layer in range(depth):
        ctx_out = layer < depth - 1
        p_l, p_c, o_f, o_b, o_at_l, o_at_c = token_mixers(
            hl, hc, w_in[layer], w_dn_conv_q[layer], w_dn_conv_kv[layer], dn_a_log[layer], dn_dt_bias[layer],
            q_norm[layer], k_norm[layer], cos2, sin2, ctx_out)
        merge_w = (dn_norm[layer], w_sc_conv[layer], mix_post[layer], ffn_pre[layer],
                   w_proj_dn[layer], w_proj_sc[layer], w_proj_attn[layer], w_out[layer])
        h_lat, f_l = merge(o_f, o_b, Lc, p_l, o_at_l, h_lat, lat_mod(layer, 2), lat_mod(layer, 3), lat_mod(layer, 4),
                           *merge_w)
        f = f_l.reshape(B * L, D)
        if ctx_out:
            h_ctx, f_c = merge(o_f, o_b, 0, p_c, o_at_c, h_ctx, ctx_mod(layer, 2), ctx_mod(layer, 3),
                               ctx_mod(layer, 4), *merge_w)
            f = jnp.concatenate([f, f_c.reshape(B * Lc, D)], axis=0)
        yg, wt = moe_routed(f, w_router[layer], b_router[layer], w_exp_gate[layer], w_exp_up[layer], w_exp_down[layer])
        shared_w = (w_sh_gate[layer], w_sh_up[layer], w_sh_down[layer])
        nxt_l = nxt_c = None
        if ctx_out:
            nxt_l = (mix_pre[layer + 1], lat_mod(layer + 1, 1), lat_mod(layer + 1, 0))
            nxt_c = (mix_pre[layer + 1], ctx_mod(layer + 1, 1), ctx_mod(layer + 1, 0))
            h_ctx, hc = combine(yg, wt, f, B * L, *shared_w, h_ctx, ctx_mod(layer, 5), ffn_post[layer], nxt_c)
        h_lat, hl = combine(yg, wt, f, 0, *shared_w, h_lat, lat_mod(layer, 5), ffn_post[layer], nxt_l)
    return h_lat


BATCH_STREAMS = 2


def kernel(x, c, ctx, c_ctx, w_mod, b_mod, mix_pre, mix_post, ffn_pre, ffn_post, w_in, w_dn_conv_q, w_dn_conv_kv, dn_a_log, dn_dt_bias, dn_norm, w_sc_conv, q_norm, k_norm, w_proj_dn, w_proj_sc, w_proj_attn, w_out, w_router, b_router, w_exp_gate, w_exp_up, w_exp_down, w_sh_gate, w_sh_up, w_sh_down):
    params = (w_mod, b_mod, mix_pre, mix_post, ffn_pre, ffn_post, w_in, w_dn_conv_q, w_dn_conv_kv, dn_a_log,
              dn_dt_bias, dn_norm, w_sc_conv, q_norm, k_norm, w_proj_dn, w_proj_sc, w_proj_attn, w_out, w_router,
              b_router, w_exp_gate, w_exp_up, w_exp_down, w_sh_gate, w_sh_up, w_sh_down)
    B = x.shape[0]
    n = BATCH_STREAMS if B % BATCH_STREAMS == 0 else 1
    step = B // n
    outs = [forward(x[i * step:(i + 1) * step], c[i * step:(i + 1) * step], ctx[i * step:(i + 1) * step], c_ctx, *params)
            for i in range(n)]
    return jnp.concatenate(outs, axis=0)
```

```python
import functools

import jax
import jax.numpy as jnp
import numpy as np
from jax import lax
from jax.experimental import pallas as pl
from jax.experimental.pallas import tpu as pltpu

F32 = jnp.float32
BF16 = jnp.bfloat16

GRID_W = 64
NORM_EPS = 1e-6
DN_HEADS = 4
DN_DK = 128
DN_DV = 128
DN_CHUNK = 64
SC_WIDTH = 512
ATTN_HEADS = 8
ATTN_KV_HEADS = 2
ATTN_GROUP = ATTN_HEADS // ATTN_KV_HEADS
ATTN_HD = 128
ROPE_THETA = 10000.0
N_EXPERTS = 64
TOP_K = 8
N_GROUPS = 8
TOPK_GROUPS = 4
ROUTED_SCALE = 2.5
MOE_BLOCK = 512
N_BRANCH = 3

DN_QK_W = DN_HEADS * DN_DK
DN_V_W = DN_HEADS * DN_DV
ATTN_Q_W = ATTN_HEADS * ATTN_HD
ATTN_KV_W = ATTN_KV_HEADS * ATTN_HD
LANES = 128


def _mm_kernel(x_ref, w_ref, o_ref):
    o_ref[...] = jnp.dot(x_ref[...], w_ref[...], preferred_element_type=F32).astype(o_ref.dtype)


def _pick_tile(n, want):
    t = min(n, want)
    while n % t:
        t //= 2
    return t


def matmul(x, w, out_dtype, tm=512, tn=1024):
    M, K = x.shape
    N = w.shape[1]
    tm = _pick_tile(M, tm)
    tn = _pick_tile(N, tn)
    return pl.pallas_call(
        _mm_kernel,
        grid=(N // tn, M // tm),
        in_specs=[pl.BlockSpec((tm, K), lambda j, i: (i, 0)),
                  pl.BlockSpec((K, tn), lambda j, i: (0, j))],
        out_specs=pl.BlockSpec((tm, tn), lambda j, i: (i, j)),
        out_shape=jax.ShapeDtypeStruct((M, N), out_dtype),
        name="matmul",
    )(x.astype(BF16), w.astype(BF16))


OFF_Q = DN_QK_W + DN_V_W + 2 * ATTN_KV_W
OFF_Z = OFF_Q + DN_QK_W
OFF_U = OFF_Z + DN_V_W
OFF_B = OFF_U + SC_WIDTH
OFF_C = OFF_B + SC_WIDTH
OFF_ATQ = OFF_C + SC_WIDTH
OFF_G = OFF_ATQ + ATTN_Q_W


def _rms(x):
    return x * lax.rsqrt(jnp.mean(x * x, axis=-1, keepdims=True) + NORM_EPS)


K_COL0 = (DN_QK_W + DN_V_W) // ATTN_HD
V_COL0 = (DN_QK_W + DN_V_W + ATTN_KV_W) // ATTN_HD


def _head_norm_rope(x, w, cos2, sin2):
    y = _rms(x) * w
    if cos2 is None:
        return y
    return y * cos2 + pltpu.roll(y, ATTN_HD // 2, 1) * sin2


def _attn_kernel(*refs, n_ctx, n_lat, rope_q):
    if n_lat:
        q_ref, qn_ref, kn_ref, cq_ref, sq_ref, kc_ref, vc_ref, kl_ref, vl_ref, ck_ref, sk_ref, o_ref, k_s, v_s = refs
    else:
        q_ref, qn_ref, kn_ref, kc_ref, vc_ref, o_ref, k_s, v_s = refs

    @pl.when(pl.program_id(2) == 0)
    def _():
        k_s[0:n_ctx, :] = _head_norm_rope(kc_ref[0].astype(F32), kn_ref[...], None, None).astype(BF16)
        v_s[:, 0:n_ctx] = vc_ref[0].astype(F32).T.astype(BF16)
        if n_lat:
            k_s[n_ctx:n_ctx + n_lat, :] = _head_norm_rope(kl_ref[0].astype(F32), kn_ref[...], ck_ref[...],
                                                          sk_ref[...]).astype(BF16)
            v_s[:, n_ctx:n_ctx + n_lat] = vl_ref[0].astype(F32).T.astype(BF16)

    tq = q_ref.shape[1]
    cos2 = cq_ref[...] if rope_q else None
    sin2 = sq_ref[...] if rope_q else None
    q = jnp.concatenate(
        [_head_norm_rope(q_ref[0, :, g * ATTN_HD:(g + 1) * ATTN_HD].astype(F32), qn_ref[...], cos2, sin2).astype(BF16)
         for g in range(ATTN_GROUP)], axis=0)
    st = lax.dot_general(k_s[...], q, (((1,), (1,)), ((), ())), preferred_element_type=F32)
    m = jnp.max(st, axis=0, keepdims=True)
    p = jnp.exp(st - m)
    l = jnp.sum(p, axis=0, keepdims=True)
    ot = jnp.dot(v_s[...], p.astype(BF16), preferred_element_type=F32) / l
    for g in range(ATTN_GROUP):
        o_ref[0, :, g * ATTN_HD:(g + 1) * ATTN_HD] = ot[:, g * tq:(g + 1) * tq].T.astype(o_ref.dtype)


def attention(p_q, p_c, p_l, q_norm_s, k_norm_p, cos2, sin2, tq=256):
    B, Lq, _ = p_q.shape
    Lc = p_c.shape[1]
    L = 0 if p_l is None else p_l.shape[1]
    tq = _pick_tile(Lq, tq)
    gw = ATTN_GROUP * ATTN_HD
    q_spec = pl.BlockSpec((1, tq, gw), lambda b, h, i: (b, i, OFF_ATQ // gw + h))
    vec = pl.BlockSpec((1, ATTN_HD), lambda b, h, i: (0, 0))
    kv = lambda n, c0: pl.BlockSpec((1, n, ATTN_HD), lambda b, h, i: (b, 0, c0 + h))
    in_specs = [q_spec, vec, vec]
    args = [p_q, q_norm_s[None], k_norm_p[None]]
    if L:
        in_specs += [pl.BlockSpec((tq, ATTN_HD), lambda b, h, i: (i, 0))] * 2
        args += [cos2, sin2]
    in_specs += [kv(Lc, K_COL0), kv(Lc, V_COL0)]
    args += [p_c, p_c]
    if L:
        in_specs += [kv(L, K_COL0), kv(L, V_COL0), pl.BlockSpec((L, ATTN_HD), lambda b, h, i: (0, 0)),
                     pl.BlockSpec((L, ATTN_HD), lambda b, h, i: (0, 0))]
        args += [p_l, p_l, cos2, sin2]
    return pl.pallas_call(
        functools.partial(_attn_kernel, n_ctx=Lc, n_lat=L, rope_q=bool(L)),
        grid=(B, ATTN_KV_HEADS, Lq // tq),
        in_specs=in_specs,
        out_specs=pl.BlockSpec((1, tq, gw), lambda b, h, i: (b, i, h)),
        out_shape=jax.ShapeDtypeStruct((B, Lq, ATTN_Q_W), BF16),
        scratch_shapes=[pltpu.VMEM((Lc + L, ATTN_HD), BF16), pltpu.VMEM((ATTN_HD, Lc + L), BF16)],
        compiler_params=pltpu.CompilerParams(dimension_semantics=("arbitrary", "arbitrary", "arbitrary"),
                                             vmem_limit_bytes=48 * 1024 * 1024),
        name="attention",
    )(*args)


def _expert_kernel(be_ref, x_ref, wg_ref, wu_ref, wd_ref, o_ref, wg_s, wu_s, wd_s):
    i = pl.program_id(0)

    @pl.when((i == 0) | (be_ref[i] != be_ref[jnp.maximum(i - 1, 0)]))
    def _():
        wg_s[...] = wg_ref[0].astype(BF16)
        wu_s[...] = wu_ref[0].astype(BF16)
        wd_s[...] = wd_ref[0].astype(BF16)

    x = x_ref[...]
    g = jnp.dot(x, wg_s[...], preferred_element_type=F32)
    u = jnp.dot(x, wu_s[...], preferred_element_type=F32)
    a = (g * jax.nn.sigmoid(g) * u).astype(BF16)
    o_ref[...] = jnp.dot(a, wd_s[...], preferred_element_type=F32).astype(o_ref.dtype)


def expert_blocks(x_sorted, block_e, wg, wu, wd, tm=MOE_BLOCK):
    n_rows, D = x_sorted.shape
    F = wg.shape[-1]
    n_blocks = n_rows // tm
    grid_spec = pltpu.PrefetchScalarGridSpec(
        num_scalar_prefetch=1,
        grid=(n_blocks,),
        in_specs=[pl.BlockSpec((tm, D), lambda i, be: (i, 0)),
                  pl.BlockSpec((1, D, F), lambda i, be: (be[i], 0, 0)),
                  pl.BlockSpec((1, D, F), lambda i, be: (be[i], 0, 0)),
                  pl.BlockSpec((1, F, D), lambda i, be: (be[i], 0, 0))],
        out_specs=pl.BlockSpec((tm, D), lambda i, be: (i, 0)),
        scratch_shapes=[pltpu.VMEM((D, F), BF16), pltpu.VMEM((D, F), BF16), pltpu.VMEM((F, D), BF16)],
    )
    return pl.pallas_call(
        _expert_kernel,
        grid_spec=grid_spec,
        out_shape=jax.ShapeDtypeStruct((n_rows, D), BF16),
        compiler_params=pltpu.CompilerParams(dimension_semantics=("arbitrary",)),
        name="expert_blocks",
    )(block_e.astype(jnp.int32), x_sorted, wg, wu, wd)


DN_STACK = DN_HEADS * DN_CHUNK
DN_UNROLL = 2
DN_SPLIT_STEPS = 3
CONV_HALO = 16


def _conv_silu(x_ref, w_ref, r0, n_rows):
    C = DN_CHUNK
    x = x_ref[0, pl.ds(r0, C), :].astype(F32)
    lo = pl.multiple_of(jnp.maximum(r0 - CONV_HALO, 0), CONV_HALO)
    hi = pl.multiple_of(jnp.minimum(r0 + C, n_rows - CONV_HALO), CONV_HALO)
    before = x_ref[0, pl.ds(lo, CONV_HALO), :][CONV_HALO - 1:CONV_HALO].astype(F32)
    after = x_ref[0, pl.ds(hi, CONV_HALO), :][0:1].astype(F32)
    before = jnp.where(r0 > 0, before, 0.0)
    after = jnp.where(r0 + C < n_rows, after, 0.0)
    rows = lax.broadcasted_iota(jnp.int32, (C, 1), 0)
    x_prev = jnp.where(rows == 0, before, pltpu.roll(x, 1, 0))
    x_next = jnp.where(rows == C - 1, after, pltpu.roll(x, C - 1, 0))
    w = w_ref[...]
    y = x_prev * w[0:1] + x * w[1:2] + x_next * w[2:3]
    return y * jax.nn.sigmoid(y)


def _stack_heads(x, heads, l2_scale=None):
    parts = []
    for h in heads:
        xh = x[:, h * DN_DK:(h + 1) * DN_DK]
        if l2_scale is not None:
            xh = xh * (lax.rsqrt(jnp.sum(xh * xh, axis=-1, keepdims=True) + NORM_EPS) * l2_scale)
        parts.append(xh)
    return jnp.concatenate(parts, axis=0)


def _delta_kernel(*refs, n_ctx, n_lat, ctx_q):
    if ctx_q:
        (kc_ref, vc_ref, qc_ref, kl_ref, vl_ref, ql_ref, wk_ref, wv_ref, wq_ref,
         gcol_ref, glast_ref, beta_ref, grow_ref, of_ref, ob_ref, s_ref) = refs
    else:
        (kc_ref, vc_ref, kl_ref, vl_ref, ql_ref, wk_ref, wv_ref, wq_ref,
         gcol_ref, glast_ref, beta_ref, grow_ref, of_ref, ob_ref, s_ref) = refs
        qc_ref = None
    C, R, H = DN_CHUNK, DN_STACK, DN_HEADS
    s_ref[...] = jnp.zeros_like(s_ref)
    row = lax.broadcasted_iota(jnp.int32, (R, R), 0)
    col = lax.broadcasted_iota(jnp.int32, (R, R), 1)
    same_head = (row // C) == (col // C)
    strict = (same_head & (col < row), same_head & (col > row))
    eye = (row == col).astype(F32)

    def make_body(k_ref, v_ref, q_ref, n_rows, row0, unroll):
        n_seq = n_rows // C
        heads = range(H)

        def prepare(d, c):
            r_in = pl.multiple_of(c * C, C)
            r0 = pl.multiple_of(row0 + c * C, C)
            ks = _stack_heads(_conv_silu(k_ref, wk_ref, r_in, n_rows), heads, 1.0)
            vs = _stack_heads(_conv_silu(v_ref, wv_ref, r_in, n_rows), heads)
            if q_ref is None:
                qs = jnp.zeros((R, DN_DK), F32)
            else:
                qs = _stack_heads(_conv_silu(q_ref, wq_ref, r_in, n_rows), heads, DN_DK ** -0.5)

            def col_stack(ref):
                blk = ref[0, pl.ds(r0, C), :]
                return jnp.concatenate([blk[:, d * H + h:d * H + h + 1] for h in heads], axis=0)

            gc, gl, bt = col_stack(gcol_ref), col_stack(glast_ref), col_stack(beta_ref)
            gr = grow_ref[0, d, pl.ds(row0 // C + c, 1), :]
            ks_b = ks.astype(BF16)
            kb = ks * bt
            decay = jnp.exp(jnp.where(strict[d], gc - gr, -jnp.inf))
            g_kk = lax.dot_general(kb.astype(BF16), ks_b, (((1,), (1,)), ((), ())), preferred_element_type=F32)
            g_qk = lax.dot_general(qs.astype(BF16), ks_b, (((1,), (1,)), ((), ())), preferred_element_type=F32)
            e_gc = jnp.exp(gc)
            return dict(d=d, r0=r0, gl=gl,
                        p=(-(g_kk * decay)).astype(BF16),
                        qk=(g_qk * (decay + eye)).astype(BF16),
                        x=jnp.concatenate([vs * bt, kb * e_gc], axis=1),
                        kdec=(ks * jnp.exp(gl - gc)).astype(BF16),
                        qg=(qs * e_gc).astype(BF16))

        def body(n, carry):
            chains = []
            for u in range(unroll):
                m = n * unroll + u
                for d in range(2):
                    chains.append(prepare(d, m if d == 0 else n_seq - 1 - m))
            for it in range(6):
                for ch in chains:
                    x = ch["x"]
                    x_hi = x.astype(BF16)
                    step = jnp.dot(ch["p"], x_hi, preferred_element_type=F32)
                    if it < DN_SPLIT_STEPS:
                        x_lo = (x - x_hi.astype(F32)).astype(BF16)
                        step = step + jnp.dot(ch["p"], x_lo, preferred_element_type=F32)
                    ch["x"] = x + step
                if it < 5:
                    for ch in chains:
                        ch["p"] = jnp.dot(ch["p"], ch["p"], preferred_element_type=F32).astype(BF16)
            for u in range(unroll):
                pair = chains[2 * u:2 * u + 2]
                us, os_ = [[] for _ in pair], [[] for _ in pair]
                for h in heads:
                    sl = slice(h * C, (h + 1) * C)
                    for ci, ch in enumerate(pair):
                        d, gl = ch["d"], ch["gl"]
                        s_old = s_ref[d, h]
                        s_b = s_old.astype(BF16)
                        u_h = ch["x"][sl, :DN_DV] - jnp.dot(ch["x"][sl, DN_DV:].astype(BF16), s_b,
                                                            preferred_element_type=F32)
                        os_[ci].append(jnp.dot(ch["qg"][sl], s_b, preferred_element_type=F32))
                        cd = jnp.exp(gl[h * C:h * C + 1, :])
                        s_ref[d, h] = s_old * cd + lax.dot_general(ch["kdec"][sl], u_h.astype(BF16),
                                                                   (((0,), (0,)), ((), ())), preferred_element_type=F32)
                        us[ci].append(u_h)
                for ci, ch in enumerate(pair):
                    o = jnp.concatenate(os_[ci], axis=0) + jnp.dot(ch["qk"], jnp.concatenate(us[ci], axis=0).astype(BF16),
                                                                   preferred_element_type=F32)
                    o_ref = of_ref if ch["d"] == 0 else ob_ref
                    for h in heads:
                        o_ref[0, pl.ds(ch["r0"], C), h * DN_DV:(h + 1) * DN_DV] = o[h * C:(h + 1) * C].astype(o_ref.dtype)
            return carry

        return body, n_seq // unroll

    for k_ref, v_ref, q_ref, n_rows, row0 in ((kc_ref, vc_ref, qc_ref, n_ctx, 0), (kl_ref, vl_ref, ql_ref, n_lat, n_ctx)):
        unroll = DN_UNROLL if (n_rows // C) % DN_UNROLL == 0 else 1
        body, trips = make_body(k_ref, v_ref, q_ref, n_rows, row0, unroll)
        lax.fori_loop(0, trips, body, 0)


def delta_rule(p_c, p_l, w_conv_kv, w_conv_q, g, beta, ctx_q):
    B, Lc = p_c.shape[:2]
    L = p_l.shape[1]
    Lt = Lc + L
    H, C, W = DN_HEADS, DN_CHUNK, DN_QK_W
    nc = Lt // C
    gch = g.reshape(B, nc, C, 2, H)
    g_f = jnp.cumsum(gch[:, :, :, 0], axis=2)
    g_b = jnp.flip(jnp.cumsum(jnp.flip(gch[:, :, :, 1], axis=2), axis=2), axis=2)
    gcum = jnp.stack([g_f, g_b], axis=3)
    gtot = jnp.broadcast_to(jnp.stack([g_f[:, :, -1:], g_b[:, :, :1]], axis=3), gcum.shape)
    gcol = gcum.reshape(B, Lt, 2 * H)
    glast = gtot.reshape(B, Lt, 2 * H)
    grow = jnp.transpose(gcum, (0, 3, 1, 4, 2)).reshape(B, 2, nc, H * C)
    bt = beta.reshape(B, Lt, 2 * H)
    colblk = lambda n, j: pl.BlockSpec((1, n, W), lambda b: (b, 0, j))
    seq = lambda w: pl.BlockSpec((1, Lt, w), lambda b: (b, 0, 0))
    full = lambda a: pl.BlockSpec(a.shape, lambda b: (0,) * a.ndim)
    wk, wv, wq = (w_conv_kv[:, :W].astype(F32), w_conv_kv[:, W:].astype(F32), w_conv_q.astype(F32))
    q_col = OFF_Q // W
    in_specs = [colblk(Lc, 0), colblk(Lc, 1)] + ([colblk(Lc, q_col)] if ctx_q else [])
    args = [p_c, p_c] + ([p_c] if ctx_q else [])
    in_specs += [colblk(L, 0), colblk(L, 1), colblk(L, q_col), full(wk), full(wv), full(wq),
                 seq(2 * H), seq(2 * H), seq(2 * H), pl.BlockSpec((1, 2, nc, H * C), lambda b: (b, 0, 0, 0))]
    args += [p_l, p_l, p_l, wk, wv, wq, gcol, glast, bt, grow]
    return pl.pallas_call(
        functools.partial(_delta_kernel, n_ctx=Lc, n_lat=L, ctx_q=ctx_q),
        grid=(B,),
        in_specs=in_specs,
        out_specs=[seq(W), seq(W)],
        out_shape=[jax.ShapeDtypeStruct((B, Lt, W), BF16)] * 2,
        scratch_shapes=[pltpu.VMEM((2, H, DN_DK, DN_DV), F32)],
        compiler_params=pltpu.CompilerParams(vmem_limit_bytes=48 * 1024 * 1024),
        name="delta_rule",
    )(*args)


ROUTE_TILE = 512


def _route_kernel(h_ref, w_ref, b_ref, idx_ref, rank_ref, wt_ref, cnt_ref, carry_ref):
    E, G, tn = N_EXPERTS, N_GROUPS, h_ref.shape[0]
    per = E // G
    neg = -jnp.inf

    @pl.when(pl.program_id(0) == 0)
    def _():
        carry_ref[...] = jnp.zeros_like(carry_ref)

    logits = lax.dot_general(w_ref[...], h_ref[...], (((1,), (1,)), ((), ())), preferred_element_type=F32)
    scores = jax.nn.sigmoid(logits)
    sel = scores + b_ref[...]
    iota_p = lax.broadcasted_iota(jnp.int32, (per, tn), 0)
    gs = []
    for g in range(G):
        blk = sel[g * per:(g + 1) * per]
        m1 = jnp.max(blk, axis=0, keepdims=True)
        i1 = jnp.min(jnp.where(blk == m1, iota_p, per), axis=0, keepdims=True)
        m2 = jnp.max(jnp.where(iota_p == i1, neg, blk), axis=0, keepdims=True)
        gs.append(m1 + m2)
    gsel = jnp.concatenate(gs, axis=0)
    iota_g = lax.broadcasted_iota(jnp.int32, (G, tn), 0)
    gpick = jnp.zeros((G, tn), F32)
    for _ in range(TOPK_GROUPS):
        m = jnp.max(gsel, axis=0, keepdims=True)
        i = jnp.min(jnp.where(gsel == m, iota_g, G), axis=0, keepdims=True)
        hit = iota_g == i
        gpick = jnp.where(hit, 1.0, gpick)
        gsel = jnp.where(hit, neg, gsel)
    emask = jnp.concatenate([jnp.broadcast_to(gpick[g:g + 1], (per, tn)) for g in range(G)], axis=0) > 0.5
    cand = jnp.where(emask, sel, neg)
    iota_e = lax.broadcasted_iota(jnp.int32, (E, tn), 0)
    picked = jnp.zeros((E, tn), F32)
    ids, pick_scores = [], []
    for _ in range(TOP_K):
        m = jnp.max(cand, axis=0, keepdims=True)
        i = jnp.min(jnp.where(cand == m, iota_e, E), axis=0, keepdims=True)
        hit = iota_e == i
        ids.append(i)
        pick_scores.append(jnp.sum(jnp.where(hit, scores, 0.0), axis=0, keepdims=True))
        picked = jnp.where(hit, 1.0, picked)
        cand = jnp.where(hit, neg, cand)
    total = pick_scores[0]
    for sc in pick_scores[1:]:
        total = total + sc
    rr = lax.broadcasted_iota(jnp.int32, (tn, tn), 0)
    cc = lax.broadcasted_iota(jnp.int32, (tn, tn), 1)
    before = (rr < cc).astype(BF16)
    rank = jnp.dot(picked.astype(BF16), before, preferred_element_type=F32) + carry_ref[:, :1]
    ranks = [jnp.sum(jnp.where(iota_e == i, rank, 0.0), axis=0, keepdims=True) for i in ids]
    idx_ref[...] = jnp.concatenate(ids, axis=0)
    rank_ref[...] = jnp.concatenate(ranks, axis=0).astype(jnp.int32)
    wt_ref[...] = jnp.concatenate([sc / (total + 1e-20) * ROUTED_SCALE for sc in pick_scores], axis=0)
    carry_ref[...] = carry_ref[...] + jnp.sum(picked, axis=1, keepdims=True)
    cnt_ref[...] = carry_ref[...]


def route(h, w_router, b_router):
    T, D = h.shape
    tn = _pick_tile(T, ROUTE_TILE)
    pick = pl.BlockSpec((TOP_K, tn), lambda i: (0, i))
    idx, rank, wts, cnt = pl.pallas_call(
        _route_kernel,
        grid=(T // tn,),
        in_specs=[pl.BlockSpec((tn, D), lambda i: (i, 0)),
                  pl.BlockSpec((N_EXPERTS, D), lambda i: (0, 0)),
                  pl.BlockSpec((N_EXPERTS, 1), lambda i: (0, 0))],
        out_specs=[pick, pick, pick, pl.BlockSpec((N_EXPERTS, LANES), lambda i: (0, 0))],
        out_shape=[jax.ShapeDtypeStruct((TOP_K, T), jnp.int32), jax.ShapeDtypeStruct((TOP_K, T), jnp.int32),
                   jax.ShapeDtypeStruct((TOP_K, T), F32), jax.ShapeDtypeStruct((N_EXPERTS, LANES), F32)],
        scratch_shapes=[pltpu.VMEM((N_EXPERTS, LANES), F32)],
        compiler_params=pltpu.CompilerParams(dimension_semantics=("arbitrary",)),
        name="route",
    )(h, w_router.T.astype(BF16), b_router.astype(F32)[:, None])
    return idx, rank, wts, cnt[:, 0].astype(jnp.int32)


ROW_TILE = 256
HALO_ROWS = 16


def _merge_kernel(of_ref, ob_ref, z_ref, u_ref, b_ref, c_ref, up_ref, cp_ref, un_ref, cn_ref, at_ref,
                  g0_ref, g1_ref, g2_ref, h_ref, gate_ref, shift_ref, scale_ref, dnn_ref, wconv_ref, post_ref,
                  pre_ref, wdn_ref, wsc_ref, wat_ref, wout_ref, hn_ref, f_ref):
    i, n_i = pl.program_id(1), pl.num_programs(1)
    tm = h_ref.shape[1]
    o = of_ref[0].astype(F32) + ob_ref[0].astype(F32)
    z = z_ref[0].astype(F32)
    parts = []
    for h in range(DN_HEADS):
        cols = slice(h * DN_DV, (h + 1) * DN_DV)
        zh = z[:, cols]
        parts.append(_rms(o[:, cols]) * dnn_ref[...] * (zh * jax.nn.sigmoid(zh)))
    y_dn = jnp.dot(jnp.concatenate(parts, axis=1).astype(BF16), wdn_ref[...], preferred_element_type=F32)

    cu = c_ref[0].astype(F32) * u_ref[0].astype(F32)
    last = HALO_ROWS - 1
    prev_row = cp_ref[0, last:last + 1, :].astype(F32) * up_ref[0, last:last + 1, :].astype(F32)
    next_row = cn_ref[0, 0:1, :].astype(F32) * un_ref[0, 0:1, :].astype(F32)
    prev_row = jnp.where(i == 0, 0.0, prev_row)
    next_row = jnp.where(i == n_i - 1, 0.0, next_row)
    rows = lax.broadcasted_iota(jnp.int32, (tm, 1), 0)
    cu_prev = jnp.where(rows == 0, prev_row, pltpu.roll(cu, 1, 0))
    cu_next = jnp.where(rows == tm - 1, next_row, pltpu.roll(cu, tm - 1, 0))
    wc = wconv_ref[...]
    conv = cu_prev * wc[0:1] + cu * wc[1:2] + cu_next * wc[2:3]
    y_sc = jnp.dot((b_ref[0].astype(F32) * conv).astype(BF16), wsc_ref[...], preferred_element_type=F32)

    y_at = jnp.dot(at_ref[0], wat_ref[...], preferred_element_type=F32)
    comb = (jax.nn.sigmoid(g0_ref[0].astype(F32)) * y_dn + jax.nn.sigmoid(g1_ref[0].astype(F32)) * y_sc
            + jax.nn.sigmoid(g2_ref[0].astype(F32)) * y_at)
    y = jnp.dot(comb.astype(BF16), wout_ref[...], preferred_element_type=F32)
    hn = h_ref[0] + gate_ref[0] * (_rms(y) * post_ref[...])
    hn_ref[0] = hn
    f_ref[0] = (_rms(hn) * pre_ref[...] * (1.0 + scale_ref[0]) + shift_ref[0]).astype(f_ref.dtype)


def merge(o_f, o_b, o_row0, p, o_attn, h, gate, shift, scale, dn_norm, w_sc_conv, mix_post, ffn_pre,
          w_proj_dn, w_proj_sc, w_proj_attn, w_out):
    B, Ls, D = h.shape
    assert D == ATTN_Q_W and OFF_G % D == 0
    tm = _pick_tile(Ls, ROW_TILE)
    assert o_row0 % tm == 0 and tm % HALO_ROWS == 0
    r0 = o_row0 // tm
    hb = tm // HALO_ROWS
    n_halo = Ls // HALO_ROWS
    W = SC_WIDTH
    col = lambda off, w: pl.BlockSpec((1, tm, w), lambda b, i: (b, i, off // w))
    prev = lambda off: pl.BlockSpec((1, HALO_ROWS, W), lambda b, i: (b, jnp.maximum(i * hb - 1, 0), off // W))
    nxt = lambda off: pl.BlockSpec((1, HALO_ROWS, W), lambda b, i: (b, jnp.minimum((i + 1) * hb, n_halo - 1), off // W))
    o_spec = pl.BlockSpec((1, tm, DN_V_W), lambda b, i: (b, r0 + i, 0))
    mod = pl.BlockSpec((1, 1, D), lambda b, i: (b, 0, 0))
    full = lambda a: pl.BlockSpec(a.shape, lambda b, i: (0,) * a.ndim)
    row = pl.BlockSpec((1, tm, D), lambda b, i: (b, i, 0))
    vecs = [dn_norm.astype(F32)[None], w_sc_conv.astype(F32), mix_post.astype(F32)[None], ffn_pre.astype(F32)[None]]
    ws = [w_proj_dn.astype(BF16), w_proj_sc.astype(BF16), w_proj_attn.astype(BF16), w_out.astype(BF16)]
    return pl.pallas_call(
        _merge_kernel,
        grid=(B, Ls // tm),
        in_specs=[o_spec, o_spec, col(OFF_Z, W), col(OFF_U, W), col(OFF_B, W), col(OFF_C, W),
                  prev(OFF_U), prev(OFF_C), nxt(OFF_U), nxt(OFF_C), row,
                  col(OFF_G, D), col(OFF_G + D, D), col(OFF_G + 2 * D, D), row, mod, mod, mod]
                 + [full(a) for a in vecs] + [full(a) for a in ws],
        out_specs=[row, row],
        out_shape=[jax.ShapeDtypeStruct((B, Ls, D), F32), jax.ShapeDtypeStruct((B, Ls, D), BF16)],
        compiler_params=pltpu.CompilerParams(vmem_limit_bytes=48 * 1024 * 1024),
        name="merge",
    )(o_f, o_b, p, p, p, p, p, p, p, p, o_attn, p, p, p, h, gate, shift, scale, *vecs, *ws)


def _combine_kernel(*refs, with_next):
    yg_ref, wt_ref, f_ref, wsg_ref, wsu_ref, wsd_ref, h_ref, gate_ref, post_ref = refs[:9]
    wt = wt_ref[...]
    acc = yg_ref[0].astype(F32) * wt[:, 0:1]
    for k in range(1, TOP_K):
        acc = acc + yg_ref[k].astype(F32) * wt[:, k:k + 1]
    x = f_ref[...]
    g = jnp.dot(x, wsg_ref[...], preferred_element_type=F32)
    u = jnp.dot(x, wsu_ref[...], preferred_element_type=F32)
    a = (g * jax.nn.sigmoid(g) * u).astype(BF16)
    y = acc + jnp.dot(a, wsd_ref[...], preferred_element_type=F32)
    hn = h_ref[...] + gate_ref[0] * (_rms(y) * post_ref[...])
    if with_next:
        pre_ref, scale_ref, shift_ref, hn_ref, nx_ref = refs[9:]
        nx_ref[...] = (_rms(hn) * pre_ref[...] * (1.0 + scale_ref[0]) + shift_ref[0]).astype(nx_ref.dtype)
    else:
        hn_ref, = refs[9:]
    hn_ref[...] = hn


def combine(yg, wt, f, tok0, ws_gate, ws_up, ws_down, h, gate, ffn_post, nxt=None):
    B, Ls, D = h.shape
    tm = _pick_tile(Ls, ROW_TILE)
    assert tok0 % tm == 0
    t0 = tok0 // tm
    per_b = Ls // tm
    F = ws_gate.shape[-1]
    tok = pl.BlockSpec((tm, D), lambda i: (t0 + i, 0))
    row = pl.BlockSpec((tm, D), lambda i: (i, 0))
    mod = pl.BlockSpec((1, 1, D), lambda i: (i // per_b, 0, 0))
    vec = pl.BlockSpec((1, D), lambda i: (0, 0))
    in_specs = [pl.BlockSpec((TOP_K, tm, D), lambda i: (0, t0 + i, 0)),
                pl.BlockSpec((tm, TOP_K), lambda i: (t0 + i, 0)), tok,
                pl.BlockSpec((D, F), lambda i: (0, 0)), pl.BlockSpec((D, F), lambda i: (0, 0)),
                pl.BlockSpec((F, D), lambda i: (0, 0)), row, mod, vec]
    args = [yg, wt, f, ws_gate.astype(BF16), ws_up.astype(BF16), ws_down.astype(BF16), h.reshape(B * Ls, D), gate,
            ffn_post.astype(F32)[None]]
    out_specs = [row]
    out_shape = [jax.ShapeDtypeStruct((B * Ls, D), F32)]
    if nxt is not None:
        in_specs += [vec, mod, mod]
        args += [nxt[0].astype(F32)[None], nxt[1], nxt[2]]
        out_specs.append(row)
        out_shape.append(jax.ShapeDtypeStruct((B * Ls, D), BF16))
    outs = pl.pallas_call(
        functools.partial(_combine_kernel, with_next=nxt is not None),
        grid=(B * Ls // tm,),
        in_specs=in_specs,
        out_specs=out_specs,
        out_shape=out_shape,
        compiler_params=pltpu.CompilerParams(vmem_limit_bytes=48 * 1024 * 1024),
        name="combine",
    )(*args)
    return (outs[0].reshape(B, Ls, D), outs[1] if nxt is not None else None)


def rmsnorm(x, w):
    xf = x.astype(F32)
    y = xf * lax.rsqrt(jnp.mean(xf * xf, axis=-1, keepdims=True) + NORM_EPS)
    return y * w.astype(F32)


def rope_tables(rows):
    row_pos = jnp.repeat(jnp.arange(rows, dtype=F32), GRID_W)
    col_pos = jnp.tile(jnp.arange(GRID_W, dtype=F32), rows)
    axis_dim = ATTN_HD // 2
    inv_freq = ROPE_THETA ** (-jnp.arange(0, axis_dim, 2, dtype=F32) / axis_dim)
    ang = jnp.concatenate([row_pos[:, None] * inv_freq, col_pos[:, None] * inv_freq], axis=-1)
    cos, sin = jnp.cos(ang), jnp.sin(ang)
    return jnp.concatenate([cos, cos], axis=-1), jnp.concatenate([-sin, sin], axis=-1)


def _deinterleave_perm(n_heads):
    base = np.concatenate([np.arange(0, ATTN_HD, 2), np.arange(1, ATTN_HD, 2)])
    return np.concatenate([h * ATTN_HD + base for h in range(n_heads)])


def prep_w_in(w_in):
    o_a = DN_QK_W + DN_V_W
    o_atk = o_a + 4 * DN_HEADS
    o_atv = o_atk + ATTN_KV_W
    o_rest = o_atv + ATTN_KV_W
    o_atq = o_rest + DN_QK_W + DN_V_W + 3 * SC_WIDTH
    o_gates = o_atq + ATTN_Q_W
    w_kv = w_in[:, :o_a]
    w_ab = w_in[:, o_a:o_atk]
    w_atk = w_in[:, o_atk:o_atv][:, _deinterleave_perm(ATTN_KV_HEADS)]
    w_atv = w_in[:, o_atv:o_rest]
    w_pre_q = w_in[:, o_rest:o_atq]
    w_atq = w_in[:, o_atq:o_gates][:, _deinterleave_perm(ATTN_HEADS)]
    w_gates = w_in[:, o_gates:]
    w_main = jnp.concatenate([w_kv, w_atk, w_atv, w_pre_q, w_atq, w_gates], axis=1).astype(BF16)
    w_ab = jnp.pad(w_ab, ((0, 0), (0, LANES - w_ab.shape[1]))).astype(BF16)
    return w_main, w_ab


KV_MAIN = OFF_Q


def decay_beta(ab, dn_a_log, dn_dt_bias):
    B, L = ab.shape[:2]
    a = ab[..., :2 * DN_HEADS].reshape(B, L, 2, DN_HEADS)
    g = -jnp.exp(dn_a_log.astype(F32)) * jax.nn.softplus(a + dn_dt_bias.astype(F32))
    beta = jax.nn.sigmoid(ab[..., 2 * DN_HEADS:4 * DN_HEADS].reshape(B, L, 2, DN_HEADS))
    return g, beta


def token_mixers(hl, hc, w_in, w_dn_conv_q, w_dn_conv_kv, dn_a_log, dn_dt_bias, q_norm, k_norm, cos2, sin2, ctx_out):
    L = cos2.shape[0]
    B = hl.shape[0] // L
    Lc = hc.shape[0] // B
    w_main, w_ab = prep_w_in(w_in)
    base = _deinterleave_perm(1)
    q_norm_s = (q_norm[base] * ATTN_HD ** -0.5).astype(F32)
    k_norm_p = k_norm[base].astype(F32)

    p_l = matmul(hl, w_main, BF16).reshape(B, L, -1)
    ab_l = matmul(hl, w_ab, F32).reshape(B, L, -1)
    p_c = matmul(hc, w_main if ctx_out else w_main[:, :KV_MAIN], BF16).reshape(B, Lc, -1)
    ab_c = matmul(hc, w_ab, F32).reshape(B, Lc, -1)

    g_c, beta_c = decay_beta(ab_c, dn_a_log, dn_dt_bias)
    g_l, beta_l = decay_beta(ab_l, dn_a_log, dn_dt_bias)
    o_f, o_b = delta_rule(p_c, p_l, w_dn_conv_kv, w_dn_conv_q, jnp.concatenate([g_c, g_l], axis=1),
                          jnp.concatenate([beta_c, beta_l], axis=1), ctx_out)

    o_at_l = attention(p_l, p_c, p_l, q_norm_s, k_norm_p, cos2, sin2)
    o_at_c = attention(p_c, p_c, None, q_norm_s, k_norm_p, None, None) if ctx_out else None
    return p_l, p_c, o_f, o_b, o_at_l, o_at_c


def moe_routed(f, w_router, b_router, w_gate, w_up, w_down):
    T, D = f.shape
    idx, rank, wts, counts = route(f, w_router, b_router)
    n_blocks = -(-(T * TOP_K) // MOE_BLOCK) + N_EXPERTS
    n_slots = n_blocks * MOE_BLOCK
    padded = (counts + MOE_BLOCK - 1) // MOE_BLOCK * MOE_BLOCK
    pad_end = jnp.cumsum(padded)
    pad_start = pad_end - padded
    experts = jnp.arange(N_EXPERTS, dtype=jnp.int32)
    dest = jnp.sum(jnp.where(idx[..., None] == experts, pad_start, 0), axis=-1) + rank
    tok = jnp.broadcast_to(jnp.arange(T, dtype=jnp.int32)[None], (TOP_K, T))
    slot_tok = jnp.full((n_slots,), T, jnp.int32).at[dest.reshape(-1)].set(tok.reshape(-1), unique_indices=True)
    block_start = jnp.arange(n_blocks, dtype=jnp.int32) * MOE_BLOCK
    block_e = jnp.minimum(jnp.sum(pad_end[None, :] <= block_start[:, None], axis=1), N_EXPERTS - 1)
    f_pad = jnp.concatenate([f, jnp.zeros((1, D), BF16)], axis=0)
    y_sorted = expert_blocks(f_pad[slot_tok], block_e, w_gate, w_up, w_down)
    return y_sorted[dest], wts.T


def forward(x, c, ctx, c_ctx, w_mod, b_mod, mix_pre, mix_post, ffn_pre, ffn_post, w_in, w_dn_conv_q, w_dn_conv_kv, dn_a_log, dn_dt_bias, dn_norm, w_sc_conv, q_norm, k_norm, w_proj_dn, w_proj_sc, w_proj_attn, w_out, w_router, b_router, w_exp_gate, w_exp_up, w_exp_down, w_sh_gate, w_sh_up, w_sh_down):
    B, L, D = x.shape
    Lc = ctx.shape[1]
    depth = w_in.shape[0]
    cos2, sin2 = rope_tables(L // GRID_W)
    silu_all = jax.nn.silu(jnp.concatenate([c, c_ctx[None]], axis=0))
    n_pad = -(-(B + 1) // 8) * 8
    silu_all = jnp.pad(silu_all, ((0, n_pad - (B + 1)), (0, 0)))
    mods = [matmul(silu_all, w_mod[layer], F32) + b_mod[layer] for layer in range(depth)]
    lat_mod = lambda layer, j: mods[layer][:B, None, j * D:(j + 1) * D]
    ctx_mod = lambda layer, j: jnp.broadcast_to(mods[layer][B, j * D:(j + 1) * D], (B, 1, D))

    h_lat, h_ctx = x, ctx
    hl = (rmsnorm(x, mix_pre[0]) * (1 + lat_mod(0, 1)) + lat_mod(0, 0)).astype(BF16).reshape(B * L, D)
    hc = (rmsnorm(ctx, mix_pre[0]) * (1 + ctx_mod(0, 1)) + ctx_mod(0, 0)).astype(BF16).reshape(B * Lc, D)
    for layer in range(depth):
        ctx_out = layer < depth - 1
        p_l, p_c, o_f, o_b, o_at_l, o_at_c = token_mixers(
            hl, hc, w_in[layer], w_dn_conv_q[layer], w_dn_conv_kv[layer], dn_a_log[layer], dn_dt_bias[layer],
            q_norm[layer], k_norm[layer], cos2, sin2, ctx_out)
        merge_w = (dn_norm[layer], w_sc_conv[layer], mix_post[layer], ffn_pre[layer],
                   w_proj_dn[layer], w_proj_sc[layer], w_proj_attn[layer], w_out[layer])
        h_lat, f_l = merge(o_f, o_b, Lc, p_l, o_at_l, h_lat, lat_mod(layer, 2), lat_mod(layer, 3), lat_mod(layer, 4),
                           *merge_w)
        f = f_l.reshape(B * L, D)
        if ctx_out:
            h_ctx, f_c = merge(o_f, o_b, 0, p_c, o_at_c, h_ctx, ctx_mod(layer, 2), ctx_mod(layer, 3),
                               ctx_mod(layer, 4), *merge_w)
            f = jnp.concatenate([f, f_c.reshape(B * Lc, D)], axis=0)
        yg, wt = moe_routed(f, w_router[layer], b_router[layer], w_exp_gate[layer], w_exp_up[layer], w_exp_down[layer])
        shared_w = (w_sh_gate[layer], w_sh_up[layer], w_sh_down[layer])
        nxt_l = nxt_c = None
        if ctx_out:
            nxt_l = (mix_pre[layer + 1], lat_mod(layer + 1, 1), lat_mod(layer + 1, 0))
            nxt_c = (mix_pre[layer + 1], ctx_mod(layer + 1, 1), ctx_mod(layer + 1, 0))
            h_ctx, hc = combine(yg, wt, f, B * L, *shared_w, h_ctx, ctx_mod(layer, 5), ffn_post[layer], nxt_c)
        h_lat, hl = combine(yg, wt, f, 0, *shared_w, h_lat, lat_mod(layer, 5), ffn_post[layer], nxt_l)
    return h_lat


BATCH_STREAMS = 2


def kernel(x, c, ctx, c_ctx, w_mod, b_mod, mix_pre, mix_post, ffn_pre, ffn_post, w_in, w_dn_conv_q, w_dn_conv_kv, dn_a_log, dn_dt_bias, dn_norm, w_sc_conv, q_norm, k_norm, w_proj_dn, w_proj_sc, w_proj_attn, w_out, w_router, b_router, w_exp_gate, w_exp_up, w_exp_down, w_sh_gate, w_sh_up, w_sh_down):
    params = (w_mod, b_mod, mix_pre, mix_post, ffn_pre, ffn_post, w_in, w_dn_conv_q, w_dn_conv_kv, dn_a_log,
              dn_dt_bias, dn_norm, w_sc_conv, q_norm, k_norm, w_proj_dn, w_proj_sc, w_proj_attn, w_out, w_router,
              b_router, w_exp_gate, w_exp_up, w_exp_down, w_sh_gate, w_sh_up, w_sh_down)
    B = x.shape[0]
    n = BATCH_STREAMS if B % BATCH_STREAMS == 0 else 1
    step = B // n
    outs = [forward(x[i * step:(i + 1) * step], c[i * step:(i + 1) * step], ctx[i * step:(i + 1) * step], c_ctx, *params)
            for i in range(n)]
    return jnp.concatenate(outs, axis=0)
```

```python
import functools

import jax
import jax.numpy as jnp
import numpy as np
from jax import lax
from jax.experimental import pallas as pl
from jax.experimental.pallas import tpu as pltpu

F32 = jnp.float32
BF16 = jnp.bfloat16

GRID_W = 64
NORM_EPS = 1e-6
DN_HEADS = 4
DN_DK = 128
DN_DV = 128
DN_CHUNK = 64
SC_WIDTH = 512
ATTN_HEADS = 8
ATTN_KV_HEADS = 2
ATTN_GROUP = ATTN_HEADS // ATTN_KV_HEADS
ATTN_HD = 128
ROPE_THETA = 10000.0
N_EXPERTS = 64
TOP_K = 8
N_GROUPS = 8
TOPK_GROUPS = 4
ROUTED_SCALE = 2.5
MOE_BLOCK = 512
N_BRANCH = 3

DN_QK_W = DN_HEADS * DN_DK
DN_V_W = DN_HEADS * DN_DV
ATTN_Q_W = ATTN_HEADS * ATTN_HD
ATTN_KV_W = ATTN_KV_HEADS * ATTN_HD
LANES = 128


def _mm_kernel(x_ref, w_ref, o_ref):
    o_ref[...] = jnp.dot(x_ref[...], w_ref[...], preferred_element_type=F32).astype(o_ref.dtype)


def _pick_tile(n, want):
    t = min(n, want)
    while n % t:
        t //= 2
    return t


def matmul(x, w, out_dtype, tm=512, tn=1024):
    M, K = x.shape
    N = w.shape[1]
    tm = _pick_tile(M, tm)
    tn = _pick_tile(N, tn)
    return pl.pallas_call(
        _mm_kernel,
        grid=(N // tn, M // tm),
        in_specs=[pl.BlockSpec((tm, K), lambda j, i: (i, 0)),
                  pl.BlockSpec((K, tn), lambda j, i: (0, j))],
        out_specs=pl.BlockSpec((tm, tn), lambda j, i: (i, j)),
        out_shape=jax.ShapeDtypeStruct((M, N), out_dtype),
        name="matmul",
    )(x.astype(BF16), w.astype(BF16))


OFF_Q = DN_QK_W + DN_V_W + 2 * ATTN_KV_W
OFF_Z = OFF_Q + DN_QK_W
OFF_U = OFF_Z + DN_V_W
OFF_B = OFF_U + SC_WIDTH
OFF_C = OFF_B + SC_WIDTH
OFF_ATQ = OFF_C + SC_WIDTH
OFF_G = OFF_ATQ + ATTN_Q_W


def _rms(x):
    return x * lax.rsqrt(jnp.mean(x * x, axis=-1, keepdims=True) + NORM_EPS)


K_COL0 = (DN_QK_W + DN_V_W) // ATTN_HD
V_COL0 = (DN_QK_W + DN_V_W + ATTN_KV_W) // ATTN_HD


def _head_norm_rope(x, w, cos2, sin2):
    y = _rms(x) * w
    if cos2 is None:
        return y
    return y * cos2 + pltpu.roll(y, ATTN_HD // 2, 1) * sin2


def _attn_kernel(*refs, n_ctx, n_lat, rope_q):
    if n_lat:
        q_ref, qn_ref, kn_ref, cq_ref, sq_ref, kc_ref, vc_ref, kl_ref, vl_ref, ck_ref, sk_ref, o_ref, k_s, v_s = refs
    else:
        q_ref, qn_ref, kn_ref, kc_ref, vc_ref, o_ref, k_s, v_s = refs

    @pl.when(pl.program_id(2) == 0)
    def _():
        k_s[0:n_ctx, :] = _head_norm_rope(kc_ref[0].astype(F32), kn_ref[...], None, None).astype(BF16)
        v_s[:, 0:n_ctx] = vc_ref[0].astype(F32).T.astype(BF16)
        if n_lat:
            k_s[n_ctx:n_ctx + n_lat, :] = _head_norm_rope(kl_ref[0].astype(F32), kn_ref[...], ck_ref[...],
                                                          sk_ref[...]).astype(BF16)
            v_s[:, n_ctx:n_ctx + n_lat] = vl_ref[0].astype(F32).T.astype(BF16)

    tq = q_ref.shape[1]
    cos2 = cq_ref[...] if rope_q else None
    sin2 = sq_ref[...] if rope_q else None
    q = jnp.concatenate(
        [_head_norm_rope(q_ref[0, :, g * ATTN_HD:(g + 1) * ATTN_HD].astype(F32), qn_ref[...], cos2, sin2).astype(BF16)
         for g in range(ATTN_GROUP)], axis=0)
    st = lax.dot_general(k_s[...], q, (((1,), (1,)), ((), ())), preferred_element_type=F32)
    m = jnp.max(st, axis=0, keepdims=True)
    p = jnp.exp(st - m)
    l = jnp.sum(p, axis=0, keepdims=True)
    ot = jnp.dot(v_s[...], p.astype(BF16), preferred_element_type=F32) / l
    for g in range(ATTN_GROUP):
        o_ref[0, :, g * ATTN_HD:(g + 1) * ATTN_HD] = ot[:, g * tq:(g + 1) * tq].T.astype(o_ref.dtype)


def attention(p_q, p_c, p_l, q_norm_s, k_norm_p, cos2, sin2, tq=256):
    B, Lq, _ = p_q.shape
    Lc = p_c.shape[1]
    L = 0 if p_l is None else p_l.shape[1]
    tq = _pick_tile(Lq, tq)
    gw = ATTN_GROUP * ATTN_HD
    q_spec = pl.BlockSpec((1, tq, gw), lambda b, h, i: (b, i, OFF_ATQ // gw + h))
    vec = pl.BlockSpec((1, ATTN_HD), lambda b, h, i: (0, 0))
    kv = lambda n, c0: pl.BlockSpec((1, n, ATTN_HD), lambda b, h, i: (b, 0, c0 + h))
    in_specs = [q_spec, vec, vec]
    args = [p_q, q_norm_s[None], k_norm_p[None]]
    if L:
        in_specs += [pl.BlockSpec((tq, ATTN_HD), lambda b, h, i: (i, 0))] * 2
        args += [cos2, sin2]
    in_specs += [kv(Lc, K_COL0), kv(Lc, V_COL0)]
    args += [p_c, p_c]
    if L:
        in_specs += [kv(L, K_COL0), kv(L, V_COL0), pl.BlockSpec((L, ATTN_HD), lambda b, h, i: (0, 0)),
                     pl.BlockSpec((L, ATTN_HD), lambda b, h, i: (0, 0))]
        args += [p_l, p_l, cos2, sin2]
    return pl.pallas_call(
        functools.partial(_attn_kernel, n_ctx=Lc, n_lat=L, rope_q=bool(L)),
        grid=(B, ATTN_KV_HEADS, Lq // tq),
        in_specs=in_specs,
        out_specs=pl.BlockSpec((1, tq, gw), lambda b, h, i: (b, i, h)),
        out_shape=jax.ShapeDtypeStruct((B, Lq, ATTN_Q_W), BF16),
        scratch_shapes=[pltpu.VMEM((Lc + L, ATTN_HD), BF16), pltpu.VMEM((ATTN_HD, Lc + L), BF16)],
        compiler_params=pltpu.CompilerParams(dimension_semantics=("arbitrary", "arbitrary", "arbitrary"),
                                             vmem_limit_bytes=48 * 1024 * 1024),
        name="attention",
    )(*args)


def _expert_kernel(be_ref, x_ref, wg_ref, wu_ref, wd_ref, o_ref, wg_s, wu_s, wd_s):
    i = pl.program_id(0)

    @pl.when((i == 0) | (be_ref[i] != be_ref[jnp.maximum(i - 1, 0)]))
    def _():
        wg_s[...] = wg_ref[0].astype(BF16)
        wu_s[...] = wu_ref[0].astype(BF16)
        wd_s[...] = wd_ref[0].astype(BF16)

    x = x_ref[...]
    g = jnp.dot(x, wg_s[...], preferred_element_type=F32)
    u = jnp.dot(x, wu_s[...], preferred_element_type=F32)
    a = (g * jax.nn.sigmoid(g) * u).astype(BF16)
    o_ref[...] = jnp.dot(a, wd_s[...], preferred_element_type=F32).astype(o_ref.dtype)


def expert_blocks(x_sorted, block_e, wg, wu, wd, tm=MOE_BLOCK):
    n_rows, D = x_sorted.shape
    F = wg.shape[-1]
    n_blocks = n_rows // tm
    grid_spec = pltpu.PrefetchScalarGridSpec(
        num_scalar_prefetch=1,
        grid=(n_blocks,),
        in_specs=[pl.BlockSpec((tm, D), lambda i, be: (i, 0)),
                  pl.BlockSpec((1, D, F), lambda i, be: (be[i], 0, 0)),
                  pl.BlockSpec((1, D, F), lambda i, be: (be[i], 0, 0)),
                  pl.BlockSpec((1, F, D), lambda i, be: (be[i], 0, 0))],
        out_specs=pl.BlockSpec((tm, D), lambda i, be: (i, 0)),
        scratch_shapes=[pltpu.VMEM((D, F), BF16), pltpu.VMEM((D, F), BF16), pltpu.VMEM((F, D), BF16)],
    )
    return pl.pallas_call(
        _expert_kernel,
        grid_spec=grid_spec,
        out_shape=jax.ShapeDtypeStruct((n_rows, D), BF16),
        compiler_params=pltpu.CompilerParams(dimension_semantics=("arbitrary",)),
        name="expert_blocks",
    )(block_e.astype(jnp.int32), x_sorted, wg, wu, wd)


DN_STACK = DN_HEADS * DN_CHUNK
DN_UNROLL = 2
DN_SPLIT_STEPS = 3
CONV_HALO = 16


def _conv_silu(x_ref, w_ref, r0, n_rows):
    C = DN_CHUNK
    x = x_ref[0, pl.ds(r0, C), :].astype(F32)
    lo = pl.multiple_of(jnp.maximum(r0 - CONV_HALO, 0), CONV_HALO)
    hi = pl.multiple_of(jnp.minimum(r0 + C, n_rows - CONV_HALO), CONV_HALO)
    before = x_ref[0, pl.ds(lo, CONV_HALO), :][CONV_HALO - 1:CONV_HALO].astype(F32)
    after = x_ref[0, pl.ds(hi, CONV_HALO), :][0:1].astype(F32)
    before = jnp.where(r0 > 0, before, 0.0)
    after = jnp.where(r0 + C < n_rows, after, 0.0)
    rows = lax.broadcasted_iota(jnp.int32, (C, 1), 0)
    x_prev = jnp.where(rows == 0, before, pltpu.roll(x, 1, 0))
    x_next = jnp.where(rows == C - 1, after, pltpu.roll(x, C - 1, 0))
    w = w_ref[...]
    y = x_prev * w[0:1] + x * w[1:2] + x_next * w[2:3]
    return y * jax.nn.sigmoid(y)


def _stack_heads(x, heads, l2_scale=None):
    parts = []
    for h in heads:
        xh = x[:, h * DN_DK:(h + 1) * DN_DK]
        if l2_scale is not None:
            xh = xh * (lax.rsqrt(jnp.sum(xh * xh, axis=-1, keepdims=True) + NORM_EPS) * l2_scale)
        parts.append(xh)
    return jnp.concatenate(parts, axis=0)


def _delta_kernel(*refs, n_ctx, n_lat, ctx_q):
    if ctx_q:
        (kc_ref, vc_ref, qc_ref, kl_ref, vl_ref, ql_ref, wk_ref, wv_ref, wq_ref,
         gcol_ref, glast_ref, beta_ref, grow_ref, of_ref, ob_ref, s_ref) = refs
    else:
        (kc_ref, vc_ref, kl_ref, vl_ref, ql_ref, wk_ref, wv_ref, wq_ref,
         gcol_ref, glast_ref, beta_ref, grow_ref, of_ref, ob_ref, s_ref) = refs
        qc_ref = None
    C, R, H = DN_CHUNK, DN_STACK, DN_HEADS
    s_ref[...] = jnp.zeros_like(s_ref)
    row = lax.broadcasted_iota(jnp.int32, (R, R), 0)
    col = lax.broadcasted_iota(jnp.int32, (R, R), 1)
    same_head = (row // C) == (col // C)
    strict = (same_head & (col < row), same_head & (col > row))
    eye = (row == col).astype(F32)

    def make_body(k_ref, v_ref, q_ref, n_rows, row0, unroll):
        n_seq = n_rows // C
        heads = range(H)

        def prepare(d, c):
            r_in = pl.multiple_of(c * C, C)
            r0 = pl.multiple_of(row0 + c * C, C)
            ks = _stack_heads(_conv_silu(k_ref, wk_ref, r_in, n_rows), heads, 1.0)
            vs = _stack_heads(_conv_silu(v_ref, wv_ref, r_in, n_rows), heads)
            if q_ref is None:
                qs = jnp.zeros((R, DN_DK), F32)
            else:
                qs = _stack_heads(_conv_silu(q_ref, wq_ref, r_in, n_rows), heads, DN_DK ** -0.5)

            def col_stack(ref):
                blk = ref[0, pl.ds(r0, C), :]
                return jnp.concatenate([blk[:, d * H + h:d * H + h + 1] for h in heads], axis=0)

            gc, gl, bt = col_stack(gcol_ref), col_stack(glast_ref), col_stack(beta_ref)
            gr = grow_ref[0, d, pl.ds(row0 // C + c, 1), :]
            ks_b = ks.astype(BF16)
            kb = ks * bt
            decay = jnp.exp(jnp.where(strict[d], gc - gr, -jnp.inf))
            g_kk = lax.dot_general(kb.astype(BF16), ks_b, (((1,), (1,)), ((), ())), preferred_element_type=F32)
            g_qk = lax.dot_general(qs.astype(BF16), ks_b, (((1,), (1,)), ((), ())), preferred_element_type=F32)
            e_gc = jnp.exp(gc)
            return dict(d=d, r0=r0, gl=gl,
                        p=(-(g_kk * decay)).astype(BF16),
                        qk=(g_qk * (decay + eye)).astype(BF16),
                        x=jnp.concatenate([vs * bt, kb * e_gc], axis=1),
                        kdec=(ks * jnp.exp(gl - gc)).astype(BF16),
                        qg=(qs * e_gc).astype(BF16))

        def body(n, carry):
            chains = []
            for u in range(unroll):
                m = n * unroll + u
                for d in range(2):
                    chains.append(prepare(d, m if d == 0 else n_seq - 1 - m))
            for it in range(6):
                for ch in chains:
                    x = ch["x"]
                    x_hi = x.astype(BF16)
                    step = jnp.dot(ch["p"], x_hi, preferred_element_type=F32)
                    if it < DN_SPLIT_STEPS:
                        x_lo = (x - x_hi.astype(F32)).astype(BF16)
                        step = step + jnp.dot(ch["p"], x_lo, preferred_element_type=F32)
                    ch["x"] = x + step
                if it < 5:
                    for ch in chains:
                        ch["p"] = jnp.dot(ch["p"], ch["p"], preferred_element_type=F32).astype(BF16)
            for u in range(unroll):
                pair = chains[2 * u:2 * u + 2]
                us, os_ = [[] for _ in pair], [[] for _ in pair]
                for h in heads:
                    sl = slice(h * C, (h + 1) * C)
                    for ci, ch in enumerate(pair):
                        d, gl = ch["d"], ch["gl"]
                        s_old = s_ref[d, h]
                        s_b = s_old.astype(BF16)
                        u_h = ch["x"][sl, :DN_DV] - jnp.dot(ch["x"][sl, DN_DV:].astype(BF16), s_b,
                                                            preferred_element_type=F32)
                        os_[ci].append(jnp.dot(ch["qg"][sl], s_b, preferred_element_type=F32))
                        cd = jnp.exp(gl[h * C:h * C + 1, :])
                        s_ref[d, h] = s_old * cd + lax.dot_general(ch["kdec"][sl], u_h.astype(BF16),
                                                                   (((0,), (0,)), ((), ())), preferred_element_type=F32)
                        us[ci].append(u_h)
                for ci, ch in enumerate(pair):
                    o = jnp.concatenate(os_[ci], axis=0) + jnp.dot(ch["qk"], jnp.concatenate(us[ci], axis=0).astype(BF16),
                                                                   preferred_element_type=F32)
                    o_ref = of_ref if ch["d"] == 0 else ob_ref
                    for h in heads:
                        o_ref[0, pl.ds(ch["r0"], C), h * DN_DV:(h + 1) * DN_DV] = o[h * C:(h + 1) * C].astype(o_ref.dtype)
            return carry

        return body, n_seq // unroll

    for k_ref, v_ref, q_ref, n_rows, row0 in ((kc_ref, vc_ref, qc_ref, n_ctx, 0), (kl_ref, vl_ref, ql_ref, n_lat, n_ctx)):
        unroll = DN_UNROLL if (n_rows // C) % DN_UNROLL == 0 else 1
        body, trips = make_body(k_ref, v_ref, q_ref, n_rows, row0, unroll)
        lax.fori_loop(0, trips, body, 0)


def delta_rule(p_c, p_l, w_conv_kv, w_conv_q, g, beta, ctx_q):
    B, Lc = p_c.shape[:2]
    L = p_l.shape[1]
    Lt = Lc + L
    H, C, W = DN_HEADS, DN_CHUNK, DN_QK_W
    nc = Lt // C
    gch = g.reshape(B, nc, C, 2, H)
    g_f = jnp.cumsum(gch[:, :, :, 0], axis=2)
    g_b = jnp.flip(jnp.cumsum(jnp.flip(gch[:, :, :, 1], axis=2), axis=2), axis=2)
    gcum = jnp.stack([g_f, g_b], axis=3)
    gtot = jnp.broadcast_to(jnp.stack([g_f[:, :, -1:], g_b[:, :, :1]], axis=3), gcum.shape)
    gcol = gcum.reshape(B, Lt, 2 * H)
    glast = gtot.reshape(B, Lt, 2 * H)
    grow = jnp.transpose(gcum, (0, 3, 1, 4, 2)).reshape(B, 2, nc, H * C)
    bt = beta.reshape(B, Lt, 2 * H)
    colblk = lambda n, j: pl.BlockSpec((1, n, W), lambda b: (b, 0, j))
    seq = lambda w: pl.BlockSpec((1, Lt, w), lambda b: (b, 0, 0))
    full = lambda a: pl.BlockSpec(a.shape, lambda b: (0,) * a.ndim)
    wk, wv, wq = (w_conv_kv[:, :W].astype(F32), w_conv_kv[:, W:].astype(F32), w_conv_q.astype(F32))
    q_col = OFF_Q // W
    in_specs = [colblk(Lc, 0), colblk(Lc, 1)] + ([colblk(Lc, q_col)] if ctx_q else [])
    args = [p_c, p_c] + ([p_c] if ctx_q else [])
    in_specs += [colblk(L, 0), colblk(L, 1), colblk(L, q_col), full(wk), full(wv), full(wq),
                 seq(2 * H), seq(2 * H), seq(2 * H), pl.BlockSpec((1, 2, nc, H * C), lambda b: (b, 0, 0, 0))]
    args += [p_l, p_l, p_l, wk, wv, wq, gcol, glast, bt, grow]
    return pl.pallas_call(
        functools.partial(_delta_kernel, n_ctx=Lc, n_lat=L, ctx_q=ctx_q),
        grid=(B,),
        in_specs=in_specs,
        out_specs=[seq(W), seq(W)],
        out_shape=[jax.ShapeDtypeStruct((B, Lt, W), BF16)] * 2,
        scratch_shapes=[pltpu.VMEM((2, H, DN_DK, DN_DV), F32)],
        compiler_params=pltpu.CompilerParams(vmem_limit_bytes=48 * 1024 * 1024),
        name="delta_rule",
    )(*args)


ROUTE_TILE = 512


def _route_kernel(h_ref, w_ref, b_ref, idx_ref, rank_ref, wt_ref, cnt_ref, carry_ref):
    E, G, tn = N_EXPERTS, N_GROUPS, h_ref.shape[0]
    per = E // G
    neg = -jnp.inf

    @pl.when(pl.program_id(0) == 0)
    def _():
        carry_ref[...] = jnp.zeros_like(carry_ref)

    logits = lax.dot_general(w_ref[...], h_ref[...], (((1,), (1,)), ((), ())), preferred_element_type=F32)
    scores = jax.nn.sigmoid(logits)
    sel = scores + b_ref[...]
    iota_p = lax.broadcasted_iota(jnp.int32, (per, tn), 0)
    gs = []
    for g in range(G):
        blk = sel[g * per:(g + 1) * per]
        m1 = jnp.max(blk, axis=0, keepdims=True)
        i1 = jnp.min(jnp.where(blk == m1, iota_p, per), axis=0, keepdims=True)
        m2 = jnp.max(jnp.where(iota_p == i1, neg, blk), axis=0, keepdims=True)
        gs.append(m1 + m2)
    gsel = jnp.concatenate(gs, axis=0)
    iota_g = lax.broadcasted_iota(jnp.int32, (G, tn), 0)
    gpick = jnp.zeros((G, tn), F32)
    for _ in range(TOPK_GROUPS):
        m = jnp.max(gsel, axis=0, keepdims=True)
        i = jnp.min(jnp.where(gsel == m, iota_g, G), axis=0, keepdims=True)
        hit = iota_g == i
        gpick = jnp.where(hit, 1.0, gpick)
        gsel = jnp.where(hit, neg, gsel)
    emask = jnp.concatenate([jnp.broadcast_to(gpick[g:g + 1], (per, tn)) for g in range(G)], axis=0) > 0.5
    cand = jnp.where(emask, sel, neg)
    iota_e = lax.broadcasted_iota(jnp.int32, (E, tn), 0)
    picked = jnp.zeros((E, tn), F32)
    ids, pick_scores = [], []
    for _ in range(TOP_K):
        m = jnp.max(cand, axis=0, keepdims=True)
        i = jnp.min(jnp.where(cand == m, iota_e, E), axis=0, keepdims=True)
        hit = iota_e == i
        ids.append(i)
        pick_scores.append(jnp.sum(jnp.where(hit, scores, 0.0), axis=0, keepdims=True))
        picked = jnp.where(hit, 1.0, picked)
        cand = jnp.where(hit, neg, cand)
    total = pick_scores[0]
    for sc in pick_scores[1:]:
        total = total + sc
    rr = lax.broadcasted_iota(jnp.int32, (tn, tn), 0)
    cc = lax.broadcasted_iota(jnp.int32, (tn, tn), 1)
    before = (rr < cc).astype(BF16)
    rank = jnp.dot(picked.astype(BF16), before, preferred_element_type=F32) + carry_ref[:, :1]
    ranks = [jnp.sum(jnp.where(iota_e == i, rank, 0.0), axis=0, keepdims=True) for i in ids]
    idx_ref[...] = jnp.concatenate(ids, axis=0)
    rank_ref[...] = jnp.concatenate(ranks, axis=0).astype(jnp.int32)
    wt_ref[...] = jnp.concatenate([sc / (total + 1e-20) * ROUTED_SCALE for sc in pick_scores], axis=0)
    carry_ref[...] = carry_ref[...] + jnp.sum(picked, axis=1, keepdims=True)
    cnt_ref[...] = carry_ref[...]


def route(h, w_router, b_router):
    T, D = h.shape
    tn = _pick_tile(T, ROUTE_TILE)
    pick = pl.BlockSpec((TOP_K, tn), lambda i: (0, i))
    idx, rank, wts, cnt = pl.pallas_call(
        _route_kernel,
        grid=(T // tn,),
        in_specs=[pl.BlockSpec((tn, D), lambda i: (i, 0)),
                  pl.BlockSpec((N_EXPERTS, D), lambda i: (0, 0)),
                  pl.BlockSpec((N_EXPERTS, 1), lambda i: (0, 0))],
        out_specs=[pick, pick, pick, pl.BlockSpec((N_EXPERTS, LANES), lambda i: (0, 0))],
        out_shape=[jax.ShapeDtypeStruct((TOP_K, T), jnp.int32), jax.ShapeDtypeStruct((TOP_K, T), jnp.int32),
                   jax.ShapeDtypeStruct((TOP_K, T), F32), jax.ShapeDtypeStruct((N_EXPERTS, LANES), F32)],
        scratch_shapes=[pltpu.VMEM((N_EXPERTS, LANES), F32)],
        compiler_params=pltpu.CompilerParams(dimension_semantics=("arbitrary",)),
        name="route",
    )(h, w_router.T.astype(BF16), b_router.astype(F32)[:, None])
    return idx, rank, wts, cnt[:, 0].astype(jnp.int32)


ROW_TILE = 256
HALO_ROWS = 16


def _merge_kernel(of_ref, ob_ref, z_ref, u_ref, b_ref, c_ref, up_ref, cp_ref, un_ref, cn_ref, at_ref,
                  g0_ref, g1_ref, g2_ref, h_ref, gate_ref, shift_ref, scale_ref, dnn_ref, wconv_ref, post_ref,
                  pre_ref, wdn_ref, wsc_ref, wat_ref, wout_ref, hn_ref, f_ref):
    i, n_i = pl.program_id(1), pl.num_programs(1)
    tm = h_ref.shape[1]
    o = of_ref[0].astype(F32) + ob_ref[0].astype(F32)
    z = z_ref[0].astype(F32)
    parts = []
    for h in range(DN_HEADS):
        cols = slice(h * DN_DV, (h + 1) * DN_DV)
        zh = z[:, cols]
        parts.append(_rms(o[:, cols]) * dnn_ref[...] * (zh * jax.nn.sigmoid(zh)))
    y_dn = jnp.dot(jnp.concatenate(parts, axis=1).astype(BF16), wdn_ref[...], preferred_element_type=F32)

    cu = c_ref[0].astype(F32) * u_ref[0].astype(F32)
    last = HALO_ROWS - 1
    prev_row = cp_ref[0, last:last + 1, :].astype(F32) * up_ref[0, last:last + 1, :].astype(F32)
    next_row = cn_ref[0, 0:1, :].astype(F32) * un_ref[0, 0:1, :].astype(F32)
    prev_row = jnp.where(i == 0, 0.0, prev_row)
    next_row = jnp.where(i == n_i - 1, 0.0, next_row)
    rows = lax.broadcasted_iota(jnp.int32, (tm, 1), 0)
    cu_prev = jnp.where(rows == 0, prev_row, pltpu.roll(cu, 1, 0))
    cu_next = jnp.where(rows == tm - 1, next_row, pltpu.roll(cu, tm - 1, 0))
    wc = wconv_ref[...]
    conv = cu_prev * wc[0:1] + cu * wc[1:2] + cu_next * wc[2:3]
    y_sc = jnp.dot((b_ref[0].astype(F32) * conv).astype(BF16), wsc_ref[...], preferred_element_type=F32)

    y_at = jnp.dot(at_ref[0], wat_ref[...], preferred_element_type=F32)
    comb = (jax.nn.sigmoid(g0_ref[0].astype(F32)) * y_dn + jax.nn.sigmoid(g1_ref[0].astype(F32)) * y_sc
            + jax.nn.sigmoid(g2_ref[0].astype(F32)) * y_at)
    y = jnp.dot(comb.astype(BF16), wout_ref[...], preferred_element_type=F32)
    hn = h_ref[0] + gate_ref[0] * (_rms(y) * post_ref[...])
    hn_ref[0] = hn
    f_ref[0] = (_rms(hn) * pre_ref[...] * (1.0 + scale_ref[0]) + shift_ref[0]).astype(f_ref.dtype)


def merge(o_f, o_b, o_row0, p, o_attn, h, gate, shift, scale, dn_norm, w_sc_conv, mix_post, ffn_pre,
          w_proj_dn, w_proj_sc, w_proj_attn, w_out):
    B, Ls, D = h.shape
    assert D == ATTN_Q_W and OFF_G % D == 0
    tm = _pick_tile(Ls, ROW_TILE)
    assert o_row0 % tm == 0 and tm % HALO_ROWS == 0
    r0 = o_row0 // tm
    hb = tm // HALO_ROWS
    n_halo = Ls // HALO_ROWS
    W = SC_WIDTH
    col = lambda off, w: pl.BlockSpec((1, tm, w), lambda b, i: (b, i, off // w))
    prev = lambda off: pl.BlockSpec((1, HALO_ROWS, W), lambda b, i: (b, jnp.maximum(i * hb - 1, 0), off // W))
    nxt = lambda off: pl.BlockSpec((1, HALO_ROWS, W), lambda b, i: (b, jnp.minimum((i + 1) * hb, n_halo - 1), off // W))
    o_spec = pl.BlockSpec((1, tm, DN_V_W), lambda b, i: (b, r0 + i, 0))
    mod = pl.BlockSpec((1, 1, D), lambda b, i: (b, 0, 0))
    full = lambda a: pl.BlockSpec(a.shape, lambda b, i: (0,) * a.ndim)
    row = pl.BlockSpec((1, tm, D), lambda b, i: (b, i, 0))
    vecs = [dn_norm.astype(F32)[None], w_sc_conv.astype(F32), mix_post.astype(F32)[None], ffn_pre.astype(F32)[None]]
    ws = [w_proj_dn.astype(BF16), w_proj_sc.astype(BF16), w_proj_attn.astype(BF16), w_out.astype(BF16)]
    return pl.pallas_call(
        _merge_kernel,
        grid=(B, Ls // tm),
        in_specs=[o_spec, o_spec, col(OFF_Z, W), col(OFF_U, W), col(OFF_B, W), col(OFF_C, W),
                  prev(OFF_U), prev(OFF_C), nxt(OFF_U), nxt(OFF_C), row,
                  col(OFF_G, D), col(OFF_G + D, D), col(OFF_G + 2 * D, D), row, mod, mod, mod]
                 + [full(a) for a in vecs] + [full(a) for a in ws],
        out_specs=[row, row],
        out_shape=[jax.ShapeDtypeStruct((B, Ls, D), F32), jax.ShapeDtypeStruct((B, Ls, D), BF16)],
        compiler_params=pltpu.CompilerParams(vmem_limit_bytes=48 * 1024 * 1024),
        name="merge",
    )(o_f, o_b, p, p, p, p, p, p, p, p, o_attn, p, p, p, h, gate, shift, scale, *vecs, *ws)


def _combine_kernel(*refs, with_next):
    yg_ref, wt_ref, f_ref, wsg_ref, wsu_ref, wsd_ref, h_ref, gate_ref, post_ref = refs[:9]
    wt = wt_ref[...]
    acc = yg_ref[0].astype(F32) * wt[:, 0:1]
    for k in range(1, TOP_K):
        acc = acc + yg_ref[k].astype(F32) * wt[:, k:k + 1]
    x = f_ref[...]
    g = jnp.dot(x, wsg_ref[...], preferred_element_type=F32)
    u = jnp.dot(x, wsu_ref[...], preferred_element_type=F32)
    a = (g * jax.nn.sigmoid(g) * u).astype(BF16)
    y = acc + jnp.dot(a, wsd_ref[...], preferred_element_type=F32)
    hn = h_ref[...] + gate_ref[0] * (_rms(y) * post_ref[...])
    if with_next:
        pre_ref, scale_ref, shift_ref, hn_ref, nx_ref = refs[9:]
        nx_ref[...] = (_rms(hn) * pre_ref[...] * (1.0 + scale_ref[0]) + shift_ref[0]).astype(nx_ref.dtype)
    else:
        hn_ref, = refs[9:]
    hn_ref[...] = hn


def combine(yg, wt, f, tok0, ws_gate, ws_up, ws_down, h, gate, ffn_post, nxt=None):
    B, Ls, D = h.shape
    tm = _pick_tile(Ls, ROW_TILE)
    assert tok0 % tm == 0
    t0 = tok0 // tm
    per_b = Ls // tm
    F = ws_gate.shape[-1]
    tok = pl.BlockSpec((tm, D), lambda i: (t0 + i, 0))
    row = pl.BlockSpec((tm, D), lambda i: (i, 0))
    mod = pl.BlockSpec((1, 1, D), lambda i: (i // per_b, 0, 0))
    vec = pl.BlockSpec((1, D), lambda i: (0, 0))
    in_specs = [pl.BlockSpec((TOP_K, tm, D), lambda i: (0, t0 + i, 0)),
                pl.BlockSpec((tm, TOP_K), lambda i: (t0 + i, 0)), tok,
                pl.BlockSpec((D, F), lambda i: (0, 0)), pl.BlockSpec((D, F), lambda i: (0, 0)),
                pl.BlockSpec((F, D), lambda i: (0, 0)), row, mod, vec]
    args = [yg, wt, f, ws_gate.astype(BF16), ws_up.astype(BF16), ws_down.astype(BF16), h.reshape(B * Ls, D), gate,
            ffn_post.astype(F32)[None]]
    out_specs = [row]
    out_shape = [jax.ShapeDtypeStruct((B * Ls, D), F32)]
    if nxt is not None:
        in_specs += [vec, mod, mod]
        args += [nxt[0].astype(F32)[None], nxt[1], nxt[2]]
        out_specs.append(row)
        out_shape.append(jax.ShapeDtypeStruct((B * Ls, D), BF16))
    outs = pl.pallas_call(
        functools.partial(_combine_kernel, with_next=nxt is not None),
        grid=(B * Ls // tm,),
        in_specs=in_specs,
        out_specs=out_specs,
        out_shape=out_shape,
        compiler_params=pltpu.CompilerParams(vmem_limit_bytes=48 * 1024 * 1024),
        name="combine",
    )(*args)
    return (outs[0].reshape(B, Ls, D), outs[1] if nxt is not None else None)


def rmsnorm(x, w):
    xf = x.astype(F32)
    y = xf * lax.rsqrt(jnp.mean(xf * xf, axis=-1, keepdims=True) + NORM_EPS)
    return y * w.astype(F32)


def rope_tables(rows):
    row_pos = jnp.repeat(jnp.arange(rows, dtype=F32), GRID_W)
    col_pos = jnp.tile(jnp.arange(GRID_W, dtype=F32), rows)
    axis_dim = ATTN_HD // 2
    inv_freq = ROPE_THETA ** (-jnp.arange(0, axis_dim, 2, dtype=F32) / axis_dim)
    ang = jnp.concatenate([row_pos[:, None] * inv_freq, col_pos[:, None] * inv_freq], axis=-1)
    cos, sin = jnp.cos(ang), jnp.sin(ang)
    return jnp.concatenate([cos, cos], axis=-1), jnp.concatenate([-sin, sin], axis=-1)


def _deinterleave_perm(n_heads):
    base = np.concatenate([np.arange(0, ATTN_HD, 2), np.arange(1, ATTN_HD, 2)])
    return np.concatenate([h * ATTN_HD + base for h in range(n_heads)])


def prep_w_in(w_in):
    o_a = DN_QK_W + DN_V_W
    o_atk = o_a + 4 * DN_HEADS
    o_atv = o_atk + ATTN_KV_W
    o_rest = o_atv + ATTN_KV_W
    o_atq = o_rest + DN_QK_W + DN_V_W + 3 * SC_WIDTH
    o_gates = o_atq + ATTN_Q_W
    w_kv = w_in[:, :o_a]
    w_ab = w_in[:, o_a:o_atk]
    w_atk = w_in[:, o_atk:o_atv][:, _deinterleave_perm(ATTN_KV_HEADS)]
    w_atv = w_in[:, o_atv:o_rest]
    w_pre_q = w_in[:, o_rest:o_atq]
    w_atq = w_in[:, o_atq:o_gates][:, _deinterleave_perm(ATTN_HEADS)]
    w_gates = w_in[:, o_gates:]
    w_main = jnp.concatenate([w_kv, w_atk, w_atv, w_pre_q, w_atq, w_gates], axis=1).astype(BF16)
    w_ab = jnp.pad(w_ab, ((0, 0), (0, LANES - w_ab.shape[1]))).astype(BF16)
    return w_main, w_ab


KV_MAIN = OFF_Q


def decay_beta(ab, dn_a_log, dn_dt_bias):
    B, L = ab.shape[:2]
    a = ab[..., :2 * DN_HEADS].reshape(B, L, 2, DN_HEADS)
    g = -jnp.exp(dn_a_log.astype(F32)) * jax.nn.softplus(a + dn_dt_bias.astype(F32))
    beta = jax.nn.sigmoid(ab[..., 2 * DN_HEADS:4 * DN_HEADS].reshape(B, L, 2, DN_HEADS))
    return g, beta


def token_mixers(hl, hc, w_in, w_dn_conv_q, w_dn_conv_kv, dn_a_log, dn_dt_bias, q_norm, k_norm, cos2, sin2, ctx_out):
    L = cos2.shape[0]
    B = hl.shape[0] // L
    Lc = hc.shape[0] // B
    w_main, w_ab = prep_w_in(w_in)
    base = _deinterleave_perm(1)
    q_norm_s = (q_norm[base] * ATTN_HD ** -0.5).astype(F32)
    k_norm_p = k_norm[base].astype(F32)

    p_l = matmul(hl, w_main, BF16).reshape(B, L, -1)
    ab_l = matmul(hl, w_ab, F32).reshape(B, L, -1)
    p_c = matmul(hc, w_main if ctx_out else w_main[:, :KV_MAIN], BF16).reshape(B, Lc, -1)
    ab_c = matmul(hc, w_ab, F32).reshape(B, Lc, -1)

    g_c, beta_c = decay_beta(ab_c, dn_a_log, dn_dt_bias)
    g_l, beta_l = decay_beta(ab_l, dn_a_log, dn_dt_bias)
    o_f, o_b = delta_rule(p_c, p_l, w_dn_conv_kv, w_dn_conv_q, jnp.concatenate([g_c, g_l], axis=1),
                          jnp.concatenate([beta_c, beta_l], axis=1), ctx_out)

    o_at_l = attention(p_l, p_c, p_l, q_norm_s, k_norm_p, cos2, sin2)
    o_at_c = attention(p_c, p_c, None, q_norm_s, k_norm_p, None, None) if ctx_out else None
    return p_l, p_c, o_f, o_b, o_at_l, o_at_c


def moe_routed(f, w_router, b_router, w_gate, w_up, w_down):
    T, D = f.shape
    idx, rank, wts, counts = route(f, w_router, b_router)
    n_blocks = -(-(T * TOP_K) // MOE_BLOCK) + N_EXPERTS
    n_slots = n_blocks * MOE_BLOCK
    padded = (counts + MOE_BLOCK - 1) // MOE_BLOCK * MOE_BLOCK
    pad_end = jnp.cumsum(padded)
    pad_start = pad_end - padded
    experts = jnp.arange(N_EXPERTS, dtype=jnp.int32)
    dest = jnp.sum(jnp.where(idx[..., None] == experts, pad_start, 0), axis=-1) + rank
    tok = jnp.broadcast_to(jnp.arange(T, dtype=jnp.int32)[None], (TOP_K, T))
    slot_tok = (jnp.arange(n_slots, dtype=jnp.int32) % T).at[dest.reshape(-1)].set(tok.reshape(-1), unique_indices=True)
    block_start = jnp.arange(n_blocks, dtype=jnp.int32) * MOE_BLOCK
    block_e = jnp.minimum(jnp.sum(pad_end[None, :] <= block_start[:, None], axis=1), N_EXPERTS - 1)
    y_sorted = expert_blocks(f[slot_tok], block_e, w_gate, w_up, w_down)
    return y_sorted[dest], wts.T


def forward(x, c, ctx, c_ctx, w_mod, b_mod, mix_pre, mix_post, ffn_pre, ffn_post, w_in, w_dn_conv_q, w_dn_conv_kv, dn_a_log, dn_dt_bias, dn_norm, w_sc_conv, q_norm, k_norm, w_proj_dn, w_proj_sc, w_proj_attn, w_out, w_router, b_router, w_exp_gate, w_exp_up, w_exp_down, w_sh_gate, w_sh_up, w_sh_down):
    B, L, D = x.shape
    Lc = ctx.shape[1]
    depth = w_in.shape[0]
    cos2, sin2 = rope_tables(L // GRID_W)
    silu_all = jax.nn.silu(jnp.concatenate([c, c_ctx[None]], axis=0))
    n_pad = -(-(B + 1) // 8) * 8
    silu_all = jnp.pad(silu_all, ((0, n_pad - (B + 1)), (0, 0)))
    mods = [matmul(silu_all, w_mod[layer], F32) + b_mod[layer] for layer in range(depth)]
    lat_mod = lambda layer, j: mods[layer][:B, None, j * D:(j + 1) * D]
    ctx_mod = lambda layer, j: jnp.broadcast_to(mods[layer][B, j * D:(j + 1) * D], (B, 1, D))

    h_lat, h_ctx = x, ctx
    hl = (rmsnorm(x, mix_pre[0]) * (1 + lat_mod(0, 1)) + lat_mod(0, 0)).astype(BF16).reshape(B * L, D)
    hc = (rmsnorm(ctx, mix_pre[0]) * (1 + ctx_mod(0, 1)) + ctx_mod(0, 0)).astype(BF16).reshape(B * Lc, D)
    for layer in range(depth):
        ctx_out = layer < depth - 1
        p_l, p_c, o_f, o_b, o_at_l, o_at_c = token_mixers(
            hl, hc, w_in[layer], w_dn_conv_q[layer], w_dn_conv_kv[layer], dn_a_log[layer], dn_dt_bias[layer],
            q_norm[layer], k_norm[layer], cos2, sin2, ctx_out)
        merge_w = (dn_norm[layer], w_sc_conv[layer], mix_post[layer], ffn_pre[layer],
                   w_proj_dn[layer], w_proj_sc[layer], w_proj_attn[layer], w_out[layer])
        h_lat, f_l = merge(o_f, o_b, Lc, p_l, o_at_l, h_lat, lat_mod(layer, 2), lat_mod(layer, 3), lat_mod(layer, 4),
                           *merge_w)
        f = f_l.reshape(B * L, D)
        if ctx_out:
            h_ctx, f_c = merge(o_f, o_b, 0, p_c, o_at_c, h_ctx, ctx_mod(layer, 2), ctx_mod(layer, 3),
                               ctx_mod(layer, 4), *merge_w)
            f = jnp.concatenate([f, f_c.reshape(B * Lc, D)], axis=0)
        yg, wt = moe_routed(f, w_router[layer], b_router[layer], w_exp_gate[layer], w_exp_up[layer], w_exp_down[layer])
        shared_w = (w_sh_gate[layer], w_sh_up[layer], w_sh_down[layer])
        nxt_l = nxt_c = None
        if ctx_out:
            nxt_l = (mix_pre[layer + 1], lat_mod(layer + 1, 1), lat_mod(layer + 1, 0))
            nxt_c = (mix_pre[layer + 1], ctx_mod(layer + 1, 1), ctx_mod(layer + 1, 0))
            h_ctx, hc = combine(yg, wt, f, B * L, *shared_w, h_ctx, ctx_mod(layer, 5), ffn_post[layer], nxt_c)
        h_lat, hl = combine(yg, wt, f, 0, *shared_w, h_lat, lat_mod(layer, 5), ffn_post[layer], nxt_l)
    return h_lat


BATCH_STREAMS = 2


def kernel(x, c, ctx, c_ctx, w_mod, b_mod, mix_pre, mix_post, ffn_pre, ffn_post, w_in, w_dn_conv_q, w_dn_conv_kv, dn_a_log, dn_dt_bias, dn_norm, w_sc_conv, q_norm, k_norm, w_proj_dn, w_proj_sc, w_proj_attn, w_out, w_router, b_router, w_exp_gate, w_exp_up, w_exp_down, w_sh_gate, w_sh_up, w_sh_down):
    params = (w_mod, b_mod, mix_pre, mix_post, ffn_pre, ffn_post, w_in, w_dn_conv_q, w_dn_conv_kv, dn_a_log,
              dn_dt_bias, dn_norm, w_sc_conv, q_norm, k_norm, w_proj_dn, w_proj_sc, w_proj_attn, w_out, w_router,
              b_router, w_exp_gate, w_exp_up, w_exp_down, w_sh_gate, w_sh_up, w_sh_down)
    B = x.shape[0]
    n = BATCH_STREAMS if B % BATCH_STREAMS == 0 else 1
    step = B // n
    outs = [forward(x[i * step:(i + 1) * step], c[i * step:(i + 1) * step], ctx[i * step:(i + 1) * step], c_ctx, *params)
            for i in range(n)]
    return jnp.concatenate(outs, axis=0)
```

```python
import functools

import jax
import jax.numpy as jnp
import numpy as np
from jax import lax
from jax.experimental import pallas as pl
from jax.experimental.pallas import tpu as pltpu

F32 = jnp.float32
BF16 = jnp.bfloat16

GRID_W = 64
NORM_EPS = 1e-6
DN_HEADS = 4
DN_DK = 128
DN_DV = 128
DN_CHUNK = 64
SC_WIDTH = 512
ATTN_HEADS = 8
ATTN_KV_HEADS = 2
ATTN_GROUP = ATTN_HEADS // ATTN_KV_HEADS
ATTN_HD = 128
ROPE_THETA = 10000.0
N_EXPERTS = 64
TOP_K = 8
N_GROUPS = 8
TOPK_GROUPS = 4
ROUTED_SCALE = 2.5
MOE_BLOCK = 512
N_BRANCH = 3

DN_QK_W = DN_HEADS * DN_DK
DN_V_W = DN_HEADS * DN_DV
ATTN_Q_W = ATTN_HEADS * ATTN_HD
ATTN_KV_W = ATTN_KV_HEADS * ATTN_HD
LANES = 128


def _mm_kernel(x_ref, w_ref, o_ref):
    o_ref[...] = jnp.dot(x_ref[...], w_ref[...], preferred_element_type=F32).astype(o_ref.dtype)


def _pick_tile(n, want):
    t = min(n, want)
    while n % t:
        t //= 2
    return t


def matmul(x, w, out_dtype, tm=512, tn=1024):
    M, K = x.shape
    N = w.shape[1]
    tm = _pick_tile(M, tm)
    tn = _pick_tile(N, tn)
    return pl.pallas_call(
        _mm_kernel,
        grid=(N // tn, M // tm),
        in_specs=[pl.BlockSpec((tm, K), lambda j, i: (i, 0)),
                  pl.BlockSpec((K, tn), lambda j, i: (0, j))],
        out_specs=pl.BlockSpec((tm, tn), lambda j, i: (i, j)),
        out_shape=jax.ShapeDtypeStruct((M, N), out_dtype),
        compiler_params=pltpu.CompilerParams(vmem_limit_bytes=40 * 1024 * 1024),
        name="matmul",
    )(x.astype(BF16), w.astype(BF16))


OFF_Q = DN_QK_W + DN_V_W + 2 * ATTN_KV_W
OFF_Z = OFF_Q + DN_QK_W
OFF_U = OFF_Z + DN_V_W
OFF_B = OFF_U + SC_WIDTH
OFF_C = OFF_B + SC_WIDTH
OFF_ATQ = OFF_C + SC_WIDTH
OFF_G = OFF_ATQ + ATTN_Q_W


def _rms(x):
    return x * lax.rsqrt(jnp.mean(x * x, axis=-1, keepdims=True) + NORM_EPS)


K_COL0 = (DN_QK_W + DN_V_W) // ATTN_HD
V_COL0 = (DN_QK_W + DN_V_W + ATTN_KV_W) // ATTN_HD


def _head_norm_rope(x, w, cos2, sin2):
    y = _rms(x) * w
    if cos2 is None:
        return y
    return y * cos2 + pltpu.roll(y, ATTN_HD // 2, 1) * sin2


def _attn_kernel(*refs, n_ctx, n_lat, rope_q):
    if n_lat:
        q_ref, qn_ref, kn_ref, cq_ref, sq_ref, kc_ref, vc_ref, kl_ref, vl_ref, ck_ref, sk_ref, o_ref, k_s, v_s = refs
    else:
        q_ref, qn_ref, kn_ref, kc_ref, vc_ref, o_ref, k_s, v_s = refs

    @pl.when(pl.program_id(2) == 0)
    def _():
        k_s[0:n_ctx, :] = _head_norm_rope(kc_ref[0].astype(F32), kn_ref[...], None, None).astype(BF16)
        v_s[:, 0:n_ctx] = vc_ref[0].astype(F32).T.astype(BF16)
        if n_lat:
            k_s[n_ctx:n_ctx + n_lat, :] = _head_norm_rope(kl_ref[0].astype(F32), kn_ref[...], ck_ref[...],
                                                          sk_ref[...]).astype(BF16)
            v_s[:, n_ctx:n_ctx + n_lat] = vl_ref[0].astype(F32).T.astype(BF16)

    tq = q_ref.shape[1]
    cos2 = cq_ref[...] if rope_q else None
    sin2 = sq_ref[...] if rope_q else None
    q = jnp.concatenate(
        [_head_norm_rope(q_ref[0, :, g * ATTN_HD:(g + 1) * ATTN_HD].astype(F32), qn_ref[...], cos2, sin2).astype(BF16)
         for g in range(ATTN_GROUP)], axis=0)
    st = lax.dot_general(k_s[...], q, (((1,), (1,)), ((), ())), preferred_element_type=F32)
    m = jnp.max(st, axis=0, keepdims=True)
    p = jnp.exp(st - m)
    l = jnp.sum(p, axis=0, keepdims=True)
    ot = jnp.dot(v_s[...], p.astype(BF16), preferred_element_type=F32) / l
    for g in range(ATTN_GROUP):
        o_ref[0, :, g * ATTN_HD:(g + 1) * ATTN_HD] = ot[:, g * tq:(g + 1) * tq].T.astype(o_ref.dtype)


def attention(p_q, p_c, p_l, q_norm_s, k_norm_p, cos2, sin2, tq=256):
    B, Lq, _ = p_q.shape
    Lc = p_c.shape[1]
    L = 0 if p_l is None else p_l.shape[1]
    tq = _pick_tile(Lq, tq)
    gw = ATTN_GROUP * ATTN_HD
    q_spec = pl.BlockSpec((1, tq, gw), lambda b, h, i: (b, i, OFF_ATQ // gw + h))
    vec = pl.BlockSpec((1, ATTN_HD), lambda b, h, i: (0, 0))
    kv = lambda n, c0: pl.BlockSpec((1, n, ATTN_HD), lambda b, h, i: (b, 0, c0 + h))
    in_specs = [q_spec, vec, vec]
    args = [p_q, q_norm_s[None], k_norm_p[None]]
    if L:
        in_specs += [pl.BlockSpec((tq, ATTN_HD), lambda b, h, i: (i, 0))] * 2
        args += [cos2, sin2]
    in_specs += [kv(Lc, K_COL0), kv(Lc, V_COL0)]
    args += [p_c, p_c]
    if L:
        in_specs += [kv(L, K_COL0), kv(L, V_COL0), pl.BlockSpec((L, ATTN_HD), lambda b, h, i: (0, 0)),
                     pl.BlockSpec((L, ATTN_HD), lambda b, h, i: (0, 0))]
        args += [p_l, p_l, cos2, sin2]
    return pl.pallas_call(
        functools.partial(_attn_kernel, n_ctx=Lc, n_lat=L, rope_q=bool(L)),
        grid=(B, ATTN_KV_HEADS, Lq // tq),
        in_specs=in_specs,
        out_specs=pl.BlockSpec((1, tq, gw), lambda b, h, i: (b, i, h)),
        out_shape=jax.ShapeDtypeStruct((B, Lq, ATTN_Q_W), BF16),
        scratch_shapes=[pltpu.VMEM((Lc + L, ATTN_HD), BF16), pltpu.VMEM((ATTN_HD, Lc + L), BF16)],
        compiler_params=pltpu.CompilerParams(dimension_semantics=("arbitrary", "arbitrary", "arbitrary"),
                                             vmem_limit_bytes=48 * 1024 * 1024),
        name="attention",
    )(*args)


def _expert_kernel(be_ref, x_ref, wg_ref, wu_ref, wd_ref, o_ref, wg_s, wu_s, wd_s):
    i = pl.program_id(0)

    @pl.when((i == 0) | (be_ref[i] != be_ref[jnp.maximum(i - 1, 0)]))
    def _():
        wg_s[...] = wg_ref[0].astype(BF16)
        wu_s[...] = wu_ref[0].astype(BF16)
        wd_s[...] = wd_ref[0].astype(BF16)

    x = x_ref[...]
    g = jnp.dot(x, wg_s[...], preferred_element_type=F32)
    u = jnp.dot(x, wu_s[...], preferred_element_type=F32)
    a = (g * jax.nn.sigmoid(g) * u).astype(BF16)
    o_ref[...] = jnp.dot(a, wd_s[...], preferred_element_type=F32).astype(o_ref.dtype)


def expert_blocks(x_sorted, block_e, wg, wu, wd, tm=MOE_BLOCK):
    n_rows, D = x_sorted.shape
    F = wg.shape[-1]
    n_blocks = n_rows // tm
    grid_spec = pltpu.PrefetchScalarGridSpec(
        num_scalar_prefetch=1,
        grid=(n_blocks,),
        in_specs=[pl.BlockSpec((tm, D), lambda i, be: (i, 0)),
                  pl.BlockSpec((1, D, F), lambda i, be: (be[i], 0, 0)),
                  pl.BlockSpec((1, D, F), lambda i, be: (be[i], 0, 0)),
                  pl.BlockSpec((1, F, D), lambda i, be: (be[i], 0, 0))],
        out_specs=pl.BlockSpec((tm, D), lambda i, be: (i, 0)),
        scratch_shapes=[pltpu.VMEM((D, F), BF16), pltpu.VMEM((D, F), BF16), pltpu.VMEM((F, D), BF16)],
    )
    return pl.pallas_call(
        _expert_kernel,
        grid_spec=grid_spec,
        out_shape=jax.ShapeDtypeStruct((n_rows, D), BF16),
        compiler_params=pltpu.CompilerParams(dimension_semantics=("arbitrary",)),
        name="expert_blocks",
    )(block_e.astype(jnp.int32), x_sorted, wg, wu, wd)


DN_STACK = DN_HEADS * DN_CHUNK
DN_UNROLL = 2
DN_SPLIT_STEPS = 3
CONV_HALO = 16


def _conv_silu(x_ref, w_ref, r0, n_rows):
    C = DN_CHUNK
    x = x_ref[0, pl.ds(r0, C), :].astype(F32)
    lo = pl.multiple_of(jnp.maximum(r0 - CONV_HALO, 0), CONV_HALO)
    hi = pl.multiple_of(jnp.minimum(r0 + C, n_rows - CONV_HALO), CONV_HALO)
    before = x_ref[0, pl.ds(lo, CONV_HALO), :][CONV_HALO - 1:CONV_HALO].astype(F32)
    after = x_ref[0, pl.ds(hi, CONV_HALO), :][0:1].astype(F32)
    before = jnp.where(r0 > 0, before, 0.0)
    after = jnp.where(r0 + C < n_rows, after, 0.0)
    rows = lax.broadcasted_iota(jnp.int32, (C, 1), 0)
    x_prev = jnp.where(rows == 0, before, pltpu.roll(x, 1, 0))
    x_next = jnp.where(rows == C - 1, after, pltpu.roll(x, C - 1, 0))
    w = w_ref[...]
    y = x_prev * w[0:1] + x * w[1:2] + x_next * w[2:3]
    return y * jax.nn.sigmoid(y)


def _stack_heads(x, heads, l2_scale=None):
    parts = []
    for h in heads:
        xh = x[:, h * DN_DK:(h + 1) * DN_DK]
        if l2_scale is not None:
            xh = xh * (lax.rsqrt(jnp.sum(xh * xh, axis=-1, keepdims=True) + NORM_EPS) * l2_scale)
        parts.append(xh)
    return jnp.concatenate(parts, axis=0)


def _delta_kernel(*refs, n_ctx, n_lat, ctx_q):
    if ctx_q:
        (kc_ref, vc_ref, qc_ref, kl_ref, vl_ref, ql_ref, wk_ref, wv_ref, wq_ref,
         gcol_ref, glast_ref, beta_ref, grow_ref, of_ref, ob_ref, s_ref) = refs
    else:
        (kc_ref, vc_ref, kl_ref, vl_ref, ql_ref, wk_ref, wv_ref, wq_ref,
         gcol_ref, glast_ref, beta_ref, grow_ref, of_ref, ob_ref, s_ref) = refs
        qc_ref = None
    C, R, H = DN_CHUNK, DN_STACK, DN_HEADS
    s_ref[...] = jnp.zeros_like(s_ref)
    row = lax.broadcasted_iota(jnp.int32, (R, R), 0)
    col = lax.broadcasted_iota(jnp.int32, (R, R), 1)
    same_head = (row // C) == (col // C)
    strict = (same_head & (col < row), same_head & (col > row))
    eye = (row == col).astype(F32)

    def make_body(k_ref, v_ref, q_ref, n_rows, row0, unroll):
        n_seq = n_rows // C
        heads = range(H)

        def prepare(d, c):
            r_in = pl.multiple_of(c * C, C)
            r0 = pl.multiple_of(row0 + c * C, C)
            ks = _stack_heads(_conv_silu(k_ref, wk_ref, r_in, n_rows), heads, 1.0)
            vs = _stack_heads(_conv_silu(v_ref, wv_ref, r_in, n_rows), heads)
            if q_ref is None:
                qs = jnp.zeros((R, DN_DK), F32)
            else:
                qs = _stack_heads(_conv_silu(q_ref, wq_ref, r_in, n_rows), heads, DN_DK ** -0.5)

            def col_stack(ref):
                blk = ref[0, pl.ds(r0, C), :]
                return jnp.concatenate([blk[:, d * H + h:d * H + h + 1] for h in heads], axis=0)

            gc, gl, bt = col_stack(gcol_ref), col_stack(glast_ref), col_stack(beta_ref)
            gr = grow_ref[0, d, pl.ds(row0 // C + c, 1), :]
            ks_b = ks.astype(BF16)
            kb = ks * bt
            decay = jnp.exp(jnp.where(strict[d], gc - gr, -jnp.inf))
            g_kk = lax.dot_general(kb.astype(BF16), ks_b, (((1,), (1,)), ((), ())), preferred_element_type=F32)
            g_qk = lax.dot_general(qs.astype(BF16), ks_b, (((1,), (1,)), ((), ())), preferred_element_type=F32)
            e_gc = jnp.exp(gc)
            return dict(d=d, r0=r0, gl=gl,
                        p=(-(g_kk * decay)).astype(BF16),
                        qk=(g_qk * (decay + eye)).astype(BF16),
                        x=jnp.concatenate([vs * bt, kb * e_gc], axis=1),
                        kdec=(ks * jnp.exp(gl - gc)).astype(BF16),
                        qg=(qs * e_gc).astype(BF16))

        def body(n, carry):
            chains = []
            for u in range(unroll):
                m = n * unroll + u
                for d in range(2):
                    chains.append(prepare(d, m if d == 0 else n_seq - 1 - m))
            for it in range(6):
                for ch in chains:
                    x = ch["x"]
                    x_hi = x.astype(BF16)
                    step = jnp.dot(ch["p"], x_hi, preferred_element_type=F32)
                    if it < DN_SPLIT_STEPS:
                        x_lo = (x - x_hi.astype(F32)).astype(BF16)
                        step = step + jnp.dot(ch["p"], x_lo, preferred_element_type=F32)
                    ch["x"] = x + step
                if it < 5:
                    for ch in chains:
                        ch["p"] = jnp.dot(ch["p"], ch["p"], preferred_element_type=F32).astype(BF16)
            for u in range(unroll):
                pair = chains[2 * u:2 * u + 2]
                us, os_ = [[] for _ in pair], [[] for _ in pair]
                for h in heads:
                    sl = slice(h * C, (h + 1) * C)
                    for ci, ch in enumerate(pair):
                        d, gl = ch["d"], ch["gl"]
                        s_old = s_ref[d, h]
                        s_b = s_old.astype(BF16)
                        u_h = ch["x"][sl, :DN_DV] - jnp.dot(ch["x"][sl, DN_DV:].astype(BF16), s_b,
                                                            preferred_element_type=F32)
                        os_[ci].append(jnp.dot(ch["qg"][sl], s_b, preferred_element_type=F32))
                        cd = jnp.exp(gl[h * C:h * C + 1, :])
                        s_ref[d, h] = s_old * cd + lax.dot_general(ch["kdec"][sl], u_h.astype(BF16),
                                                                   (((0,), (0,)), ((), ())), preferred_element_type=F32)
                        us[ci].append(u_h)
                for ci, ch in enumerate(pair):
                    o = jnp.concatenate(os_[ci], axis=0) + jnp.dot(ch["qk"], jnp.concatenate(us[ci], axis=0).astype(BF16),
                                                                   preferred_element_type=F32)
                    o_ref = of_ref if ch["d"] == 0 else ob_ref
                    for h in heads:
                        o_ref[0, pl.ds(ch["r0"], C), h * DN_DV:(h + 1) * DN_DV] = o[h * C:(h + 1) * C].astype(o_ref.dtype)
            return carry

        return body, n_seq // unroll

    for k_ref, v_ref, q_ref, n_rows, row0 in ((kc_ref, vc_ref, qc_ref, n_ctx, 0), (kl_ref, vl_ref, ql_ref, n_lat, n_ctx)):
        unroll = DN_UNROLL if (n_rows // C) % DN_UNROLL == 0 else 1
        body, trips = make_body(k_ref, v_ref, q_ref, n_rows, row0, unroll)
        lax.fori_loop(0, trips, body, 0)


def delta_rule(p_c, p_l, w_conv_kv, w_conv_q, g, beta, ctx_q):
    B, Lc = p_c.shape[:2]
    L = p_l.shape[1]
    Lt = Lc + L
    H, C, W = DN_HEADS, DN_CHUNK, DN_QK_W
    nc = Lt // C
    gch = g.reshape(B, nc, C, 2, H)
    g_f = jnp.cumsum(gch[:, :, :, 0], axis=2)
    g_b = jnp.flip(jnp.cumsum(jnp.flip(gch[:, :, :, 1], axis=2), axis=2), axis=2)
    gcum = jnp.stack([g_f, g_b], axis=3)
    gtot = jnp.broadcast_to(jnp.stack([g_f[:, :, -1:], g_b[:, :, :1]], axis=3), gcum.shape)
    gcol = gcum.reshape(B, Lt, 2 * H)
    glast = gtot.reshape(B, Lt, 2 * H)
    grow = jnp.transpose(gcum, (0, 3, 1, 4, 2)).reshape(B, 2, nc, H * C)
    bt = beta.reshape(B, Lt, 2 * H)
    colblk = lambda n, j: pl.BlockSpec((1, n, W), lambda b: (b, 0, j))
    seq = lambda w: pl.BlockSpec((1, Lt, w), lambda b: (b, 0, 0))
    full = lambda a: pl.BlockSpec(a.shape, lambda b: (0,) * a.ndim)
    wk, wv, wq = (w_conv_kv[:, :W].astype(F32), w_conv_kv[:, W:].astype(F32), w_conv_q.astype(F32))
    q_col = OFF_Q // W
    in_specs = [colblk(Lc, 0), colblk(Lc, 1)] + ([colblk(Lc, q_col)] if ctx_q else [])
    args = [p_c, p_c] + ([p_c] if ctx_q else [])
    in_specs += [colblk(L, 0), colblk(L, 1), colblk(L, q_col), full(wk), full(wv), full(wq),
                 seq(2 * H), seq(2 * H), seq(2 * H), pl.BlockSpec((1, 2, nc, H * C), lambda b: (b, 0, 0, 0))]
    args += [p_l, p_l, p_l, wk, wv, wq, gcol, glast, bt, grow]
    return pl.pallas_call(
        functools.partial(_delta_kernel, n_ctx=Lc, n_lat=L, ctx_q=ctx_q),
        grid=(B,),
        in_specs=in_specs,
        out_specs=[seq(W), seq(W)],
        out_shape=[jax.ShapeDtypeStruct((B, Lt, W), BF16)] * 2,
        scratch_shapes=[pltpu.VMEM((2, H, DN_DK, DN_DV), F32)],
        compiler_params=pltpu.CompilerParams(vmem_limit_bytes=48 * 1024 * 1024),
        name="delta_rule",
    )(*args)


ROUTE_TILE = 512


def _route_kernel(h_ref, w_ref, b_ref, idx_ref, rank_ref, wt_ref, cnt_ref, carry_ref):
    E, G, tn = N_EXPERTS, N_GROUPS, h_ref.shape[0]
    per = E // G
    neg = -jnp.inf

    @pl.when(pl.program_id(0) == 0)
    def _():
        carry_ref[...] = jnp.zeros_like(carry_ref)

    logits = lax.dot_general(w_ref[...], h_ref[...], (((1,), (1,)), ((), ())), preferred_element_type=F32)
    scores = jax.nn.sigmoid(logits)
    sel = scores + b_ref[...]
    iota_p = lax.broadcasted_iota(jnp.int32, (per, tn), 0)
    gs = []
    for g in range(G):
        blk = sel[g * per:(g + 1) * per]
        m1 = jnp.max(blk, axis=0, keepdims=True)
        i1 = jnp.min(jnp.where(blk == m1, iota_p, per), axis=0, keepdims=True)
        m2 = jnp.max(jnp.where(iota_p == i1, neg, blk), axis=0, keepdims=True)
        gs.append(m1 + m2)
    gsel = jnp.concatenate(gs, axis=0)
    iota_g = lax.broadcasted_iota(jnp.int32, (G, tn), 0)
    gpick = jnp.zeros((G, tn), F32)
    for _ in range(TOPK_GROUPS):
        m = jnp.max(gsel, axis=0, keepdims=True)
        i = jnp.min(jnp.where(gsel == m, iota_g, G), axis=0, keepdims=True)
        hit = iota_g == i
        gpick = jnp.where(hit, 1.0, gpick)
        gsel = jnp.where(hit, neg, gsel)
    emask = jnp.concatenate([jnp.broadcast_to(gpick[g:g + 1], (per, tn)) for g in range(G)], axis=0) > 0.5
    cand = jnp.where(emask, sel, neg)
    iota_e = lax.broadcasted_iota(jnp.int32, (E, tn), 0)
    picked = jnp.zeros((E, tn), F32)
    ids, pick_scores = [], []
    for _ in range(TOP_K):
        m = jnp.max(cand, axis=0, keepdims=True)
        i = jnp.min(jnp.where(cand == m, iota_e, E), axis=0, keepdims=True)
        hit = iota_e == i
        ids.append(i)
        pick_scores.append(jnp.sum(jnp.where(hit, scores, 0.0), axis=0, keepdims=True))
        picked = jnp.where(hit, 1.0, picked)
        cand = jnp.where(hit, neg, cand)
    total = pick_scores[0]
    for sc in pick_scores[1:]:
        total = total + sc
    rr = lax.broadcasted_iota(jnp.int32, (tn, tn), 0)
    cc = lax.broadcasted_iota(jnp.int32, (tn, tn), 1)
    before = (rr < cc).astype(BF16)
    rank = jnp.dot(picked.astype(BF16), before, preferred_element_type=F32) + carry_ref[:, :1]
    ranks = [jnp.sum(jnp.where(iota_e == i, rank, 0.0), axis=0, keepdims=True) for i in ids]
    idx_ref[...] = jnp.concatenate(ids, axis=0)
    rank_ref[...] = jnp.concatenate(ranks, axis=0).astype(jnp.int32)
    wt_ref[...] = jnp.concatenate([sc / (total + 1e-20) * ROUTED_SCALE for sc in pick_scores], axis=0)
    carry_ref[...] = carry_ref[...] + jnp.sum(picked, axis=1, keepdims=True)
    cnt_ref[...] = carry_ref[...]


def route(h, w_router, b_router):
    T, D = h.shape
    tn = _pick_tile(T, ROUTE_TILE)
    pick = pl.BlockSpec((TOP_K, tn), lambda i: (0, i))
    idx, rank, wts, cnt = pl.pallas_call(
        _route_kernel,
        grid=(T // tn,),
        in_specs=[pl.BlockSpec((tn, D), lambda i: (i, 0)),
                  pl.BlockSpec((N_EXPERTS, D), lambda i: (0, 0)),
                  pl.BlockSpec((N_EXPERTS, 1), lambda i: (0, 0))],
        out_specs=[pick, pick, pick, pl.BlockSpec((N_EXPERTS, LANES), lambda i: (0, 0))],
        out_shape=[jax.ShapeDtypeStruct((TOP_K, T), jnp.int32), jax.ShapeDtypeStruct((TOP_K, T), jnp.int32),
                   jax.ShapeDtypeStruct((TOP_K, T), F32), jax.ShapeDtypeStruct((N_EXPERTS, LANES), F32)],
        scratch_shapes=[pltpu.VMEM((N_EXPERTS, LANES), F32)],
        compiler_params=pltpu.CompilerParams(dimension_semantics=("arbitrary",)),
        name="route",
    )(h, w_router.T.astype(BF16), b_router.astype(F32)[:, None])
    return idx, rank, wts, cnt[:, 0].astype(jnp.int32)


ROW_TILE = 256
HALO_ROWS = 16


def _merge_kernel(of_ref, ob_ref, z_ref, u_ref, b_ref, c_ref, up_ref, cp_ref, un_ref, cn_ref, at_ref,
                  g0_ref, g1_ref, g2_ref, h_ref, gate_ref, shift_ref, scale_ref, dnn_ref, wconv_ref, post_ref,
                  pre_ref, wdn_ref, wsc_ref, wat_ref, wout_ref, hn_ref, f_ref):
    i, n_i = pl.program_id(1), pl.num_programs(1)
    tm = h_ref.shape[1]
    o = of_ref[0].astype(F32) + ob_ref[0].astype(F32)
    z = z_ref[0].astype(F32)
    parts = []
    for h in range(DN_HEADS):
        cols = slice(h * DN_DV, (h + 1) * DN_DV)
        zh = z[:, cols]
        parts.append(_rms(o[:, cols]) * dnn_ref[...] * (zh * jax.nn.sigmoid(zh)))
    y_dn = jnp.dot(jnp.concatenate(parts, axis=1).astype(BF16), wdn_ref[...], preferred_element_type=F32)

    cu = c_ref[0].astype(F32) * u_ref[0].astype(F32)
    last = HALO_ROWS - 1
    prev_row = cp_ref[0, last:last + 1, :].astype(F32) * up_ref[0, last:last + 1, :].astype(F32)
    next_row = cn_ref[0, 0:1, :].astype(F32) * un_ref[0, 0:1, :].astype(F32)
    prev_row = jnp.where(i == 0, 0.0, prev_row)
    next_row = jnp.where(i == n_i - 1, 0.0, next_row)
    rows = lax.broadcasted_iota(jnp.int32, (tm, 1), 0)
    cu_prev = jnp.where(rows == 0, prev_row, pltpu.roll(cu, 1, 0))
    cu_next = jnp.where(rows == tm - 1, next_row, pltpu.roll(cu, tm - 1, 0))
    wc = wconv_ref[...]
    conv = cu_prev * wc[0:1] + cu * wc[1:2] + cu_next * wc[2:3]
    y_sc = jnp.dot((b_ref[0].astype(F32) * conv).astype(BF16), wsc_ref[...], preferred_element_type=F32)

    y_at = jnp.dot(at_ref[0], wat_ref[...], preferred_element_type=F32)
    comb = (jax.nn.sigmoid(g0_ref[0].astype(F32)) * y_dn + jax.nn.sigmoid(g1_ref[0].astype(F32)) * y_sc
            + jax.nn.sigmoid(g2_ref[0].astype(F32)) * y_at)
    y = jnp.dot(comb.astype(BF16), wout_ref[...], preferred_element_type=F32)
    hn = h_ref[0] + gate_ref[0] * (_rms(y) * post_ref[...])
    hn_ref[0] = hn
    f_ref[0] = (_rms(hn) * pre_ref[...] * (1.0 + scale_ref[0]) + shift_ref[0]).astype(f_ref.dtype)


def merge(o_f, o_b, o_row0, p, o_attn, h, gate, shift, scale, dn_norm, w_sc_conv, mix_post, ffn_pre,
          w_proj_dn, w_proj_sc, w_proj_attn, w_out):
    B, Ls, D = h.shape
    assert D == ATTN_Q_W and OFF_G % D == 0
    tm = _pick_tile(Ls, ROW_TILE)
    assert o_row0 % tm == 0 and tm % HALO_ROWS == 0
    r0 = o_row0 // tm
    hb = tm // HALO_ROWS
    n_halo = Ls // HALO_ROWS
    W = SC_WIDTH
    col = lambda off, w: pl.BlockSpec((1, tm, w), lambda b, i: (b, i, off // w))
    prev = lambda off: pl.BlockSpec((1, HALO_ROWS, W), lambda b, i: (b, jnp.maximum(i * hb - 1, 0), off // W))
    nxt = lambda off: pl.BlockSpec((1, HALO_ROWS, W), lambda b, i: (b, jnp.minimum((i + 1) * hb, n_halo - 1), off // W))
    o_spec = pl.BlockSpec((1, tm, DN_V_W), lambda b, i: (b, r0 + i, 0))
    mod = pl.BlockSpec((1, 1, D), lambda b, i: (b, 0, 0))
    full = lambda a: pl.BlockSpec(a.shape, lambda b, i: (0,) * a.ndim)
    row = pl.BlockSpec((1, tm, D), lambda b, i: (b, i, 0))
    vecs = [dn_norm.astype(F32)[None], w_sc_conv.astype(F32), mix_post.astype(F32)[None], ffn_pre.astype(F32)[None]]
    ws = [w_proj_dn.astype(BF16), w_proj_sc.astype(BF16), w_proj_attn.astype(BF16), w_out.astype(BF16)]
    return pl.pallas_call(
        _merge_kernel,
        grid=(B, Ls // tm),
        in_specs=[o_spec, o_spec, col(OFF_Z, W), col(OFF_U, W), col(OFF_B, W), col(OFF_C, W),
                  prev(OFF_U), prev(OFF_C), nxt(OFF_U), nxt(OFF_C), row,
                  col(OFF_G, D), col(OFF_G + D, D), col(OFF_G + 2 * D, D), row, mod, mod, mod]
                 + [full(a) for a in vecs] + [full(a) for a in ws],
        out_specs=[row, row],
        out_shape=[jax.ShapeDtypeStruct((B, Ls, D), F32), jax.ShapeDtypeStruct((B, Ls, D), BF16)],
        compiler_params=pltpu.CompilerParams(vmem_limit_bytes=48 * 1024 * 1024),
        name="merge",
    )(o_f, o_b, p, p, p, p, p, p, p, p, o_attn, p, p, p, h, gate, shift, scale, *vecs, *ws)


def _combine_kernel(*refs, with_next):
    yg_ref, wt_ref, f_ref, wsg_ref, wsu_ref, wsd_ref, h_ref, gate_ref, post_ref = refs[:9]
    wt = wt_ref[...]
    acc = yg_ref[0].astype(F32) * wt[:, 0:1]
    for k in range(1, TOP_K):
        acc = acc + yg_ref[k].astype(F32) * wt[:, k:k + 1]
    x = f_ref[...]
    g = jnp.dot(x, wsg_ref[...], preferred_element_type=F32)
    u = jnp.dot(x, wsu_ref[...], preferred_element_type=F32)
    a = (g * jax.nn.sigmoid(g) * u).astype(BF16)
    y = acc + jnp.dot(a, wsd_ref[...], preferred_element_type=F32)
    hn = h_ref[...] + gate_ref[0] * (_rms(y) * post_ref[...])
    if with_next:
        pre_ref, scale_ref, shift_ref, hn_ref, nx_ref = refs[9:]
        nx_ref[...] = (_rms(hn) * pre_ref[...] * (1.0 + scale_ref[0]) + shift_ref[0]).astype(nx_ref.dtype)
    else:
        hn_ref, = refs[9:]
    hn_ref[...] = hn


def combine(yg, wt, f, tok0, ws_gate, ws_up, ws_down, h, gate, ffn_post, nxt=None):
    B, Ls, D = h.shape
    tm = _pick_tile(Ls, ROW_TILE)
    assert tok0 % tm == 0
    t0 = tok0 // tm
    per_b = Ls // tm
    F = ws_gate.shape[-1]
    tok = pl.BlockSpec((tm, D), lambda i: (t0 + i, 0))
    row = pl.BlockSpec((tm, D), lambda i: (i, 0))
    mod = pl.BlockSpec((1, 1, D), lambda i: (i // per_b, 0, 0))
    vec = pl.BlockSpec((1, D), lambda i: (0, 0))
    in_specs = [pl.BlockSpec((TOP_K, tm, D), lambda i: (0, t0 + i, 0)),
                pl.BlockSpec((tm, TOP_K), lambda i: (t0 + i, 0)), tok,
                pl.BlockSpec((D, F), lambda i: (0, 0)), pl.BlockSpec((D, F), lambda i: (0, 0)),
                pl.BlockSpec((F, D), lambda i: (0, 0)), row, mod, vec]
    args = [yg, wt, f, ws_gate.astype(BF16), ws_up.astype(BF16), ws_down.astype(BF16), h.reshape(B * Ls, D), gate,
            ffn_post.astype(F32)[None]]
    out_specs = [row]
    out_shape = [jax.ShapeDtypeStruct((B * Ls, D), F32)]
    if nxt is not None:
        in_specs += [vec, mod, mod]
        args += [nxt[0].astype(F32)[None], nxt[1], nxt[2]]
        out_specs.append(row)
        out_shape.append(jax.ShapeDtypeStruct((B * Ls, D), BF16))
    outs = pl.pallas_call(
        functools.partial(_combine_kernel, with_next=nxt is not None),
        grid=(B * Ls // tm,),
        in_specs=in_specs,
        out_specs=out_specs,
        out_shape=out_shape,
        compiler_params=pltpu.CompilerParams(vmem_limit_bytes=48 * 1024 * 1024),
        name="combine",
    )(*args)
    return (outs[0].reshape(B, Ls, D), outs[1] if nxt is not None else None)


def rmsnorm(x, w):
    xf = x.astype(F32)
    y = xf * lax.rsqrt(jnp.mean(xf * xf, axis=-1, keepdims=True) + NORM_EPS)
    return y * w.astype(F32)


def rope_tables(rows):
    row_pos = jnp.repeat(jnp.arange(rows, dtype=F32), GRID_W)
    col_pos = jnp.tile(jnp.arange(GRID_W, dtype=F32), rows)
    axis_dim = ATTN_HD // 2
    inv_freq = ROPE_THETA ** (-jnp.arange(0, axis_dim, 2, dtype=F32) / axis_dim)
    ang = jnp.concatenate([row_pos[:, None] * inv_freq, col_pos[:, None] * inv_freq], axis=-1)
    cos, sin = jnp.cos(ang), jnp.sin(ang)
    return jnp.concatenate([cos, cos], axis=-1), jnp.concatenate([-sin, sin], axis=-1)


def _deinterleave_perm(n_heads):
    base = np.concatenate([np.arange(0, ATTN_HD, 2), np.arange(1, ATTN_HD, 2)])
    return np.concatenate([h * ATTN_HD + base for h in range(n_heads)])


def prep_w_in(w_in):
    o_a = DN_QK_W + DN_V_W
    o_atk = o_a + 4 * DN_HEADS
    o_atv = o_atk + ATTN_KV_W
    o_rest = o_atv + ATTN_KV_W
    o_atq = o_rest + DN_QK_W + DN_V_W + 3 * SC_WIDTH
    o_gates = o_atq + ATTN_Q_W
    w_kv = w_in[:, :o_a]
    w_ab = w_in[:, o_a:o_atk]
    w_atk = w_in[:, o_atk:o_atv][:, _deinterleave_perm(ATTN_KV_HEADS)]
    w_atv = w_in[:, o_atv:o_rest]
    w_pre_q = w_in[:, o_rest:o_atq]
    w_atq = w_in[:, o_atq:o_gates][:, _deinterleave_perm(ATTN_HEADS)]
    w_gates = w_in[:, o_gates:]
    w_main = jnp.concatenate([w_kv, w_atk, w_atv, w_pre_q, w_atq, w_gates], axis=1).astype(BF16)
    w_ab = jnp.pad(w_ab, ((0, 0), (0, LANES - w_ab.shape[1]))).astype(BF16)
    return w_main, w_ab


KV_MAIN = OFF_Q


def decay_beta(ab, dn_a_log, dn_dt_bias):
    B, L = ab.shape[:2]
    a = ab[..., :2 * DN_HEADS].reshape(B, L, 2, DN_HEADS)
    g = -jnp.exp(dn_a_log.astype(F32)) * jax.nn.softplus(a + dn_dt_bias.astype(F32))
    beta = jax.nn.sigmoid(ab[..., 2 * DN_HEADS:4 * DN_HEADS].reshape(B, L, 2, DN_HEADS))
    return g, beta


def token_mixers(hl, hc, w_in, w_dn_conv_q, w_dn_conv_kv, dn_a_log, dn_dt_bias, q_norm, k_norm, cos2, sin2, ctx_out):
    L = cos2.shape[0]
    B = hl.shape[0] // L
    Lc = hc.shape[0] // B
    w_main, w_ab = prep_w_in(w_in)
    base = _deinterleave_perm(1)
    q_norm_s = (q_norm[base] * ATTN_HD ** -0.5).astype(F32)
    k_norm_p = k_norm[base].astype(F32)

    p_l = matmul(hl, w_main, BF16, tm=1024, tn=1024).reshape(B, L, -1)
    ab_l = matmul(hl, w_ab, F32).reshape(B, L, -1)
    p_c = matmul(hc, w_main if ctx_out else w_main[:, :KV_MAIN], BF16).reshape(B, Lc, -1)
    ab_c = matmul(hc, w_ab, F32).reshape(B, Lc, -1)

    g_c, beta_c = decay_beta(ab_c, dn_a_log, dn_dt_bias)
    g_l, beta_l = decay_beta(ab_l, dn_a_log, dn_dt_bias)
    o_f, o_b = delta_rule(p_c, p_l, w_dn_conv_kv, w_dn_conv_q, jnp.concatenate([g_c, g_l], axis=1),
                          jnp.concatenate([beta_c, beta_l], axis=1), ctx_out)

    o_at_l = attention(p_l, p_c, p_l, q_norm_s, k_norm_p, cos2, sin2)
    o_at_c = attention(p_c, p_c, None, q_norm_s, k_norm_p, None, None) if ctx_out else None
    return p_l, p_c, o_f, o_b, o_at_l, o_at_c


def moe_routed(f, w_router, b_router, w_gate, w_up, w_down):
    T, D = f.shape
    idx, rank, wts, counts = route(f, w_router, b_router)
    n_blocks = -(-(T * TOP_K) // MOE_BLOCK) + N_EXPERTS
    n_slots = n_blocks * MOE_BLOCK
    padded = (counts + MOE_BLOCK - 1) // MOE_BLOCK * MOE_BLOCK
    pad_end = jnp.cumsum(padded)
    pad_start = pad_end - padded
    experts = jnp.arange(N_EXPERTS, dtype=jnp.int32)
    dest = jnp.sum(jnp.where(idx[..., None] == experts, pad_start, 0), axis=-1) + rank
    tok = jnp.broadcast_to(jnp.arange(T, dtype=jnp.int32)[None], (TOP_K, T))
    slot_tok = (jnp.arange(n_slots, dtype=jnp.int32) % T).at[dest.reshape(-1)].set(tok.reshape(-1), unique_indices=True)
    block_start = jnp.arange(n_blocks, dtype=jnp.int32) * MOE_BLOCK
    block_e = jnp.minimum(jnp.sum(pad_end[None, :] <= block_start[:, None], axis=1), N_EXPERTS - 1)
    y_sorted = expert_blocks(f[slot_tok], block_e, w_gate, w_up, w_down)
    return y_sorted[dest], wts.T


def forward(x, c, ctx, c_ctx, w_mod, b_mod, mix_pre, mix_post, ffn_pre, ffn_post, w_in, w_dn_conv_q, w_dn_conv_kv, dn_a_log, dn_dt_bias, dn_norm, w_sc_conv, q_norm, k_norm, w_proj_dn, w_proj_sc, w_proj_attn, w_out, w_router, b_router, w_exp_gate, w_exp_up, w_exp_down, w_sh_gate, w_sh_up, w_sh_down):
    B, L, D = x.shape
    Lc = ctx.shape[1]
    depth = w_in.shape[0]
    cos2, sin2 = rope_tables(L // GRID_W)
    silu_all = jax.nn.silu(jnp.concatenate([c, c_ctx[None]], axis=0))
    n_pad = -(-(B + 1) // 8) * 8
    silu_all = jnp.pad(silu_all, ((0, n_pad - (B + 1)), (0, 0)))
    mods = [matmul(silu_all, w_mod[layer], F32) + b_mod[layer] for layer in range(depth)]
    lat_mod = lambda layer, j: mods[layer][:B, None, j * D:(j + 1) * D]
    ctx_mod = lambda layer, j: jnp.broadcast_to(mods[layer][B, j * D:(j + 1) * D], (B, 1, D))

    h_lat, h_ctx = x, ctx
    hl = (rmsnorm(x, mix_pre[0]) * (1 + lat_mod(0, 1)) + lat_mod(0, 0)).astype(BF16).reshape(B * L, D)
    hc = (rmsnorm(ctx, mix_pre[0]) * (1 + ctx_mod(0, 1)) + ctx_mod(0, 0)).astype(BF16).reshape(B * Lc, D)
    for layer in range(depth):
        ctx_out = layer < depth - 1
        p_l, p_c, o_f, o_b, o_at_l, o_at_c = token_mixers(
            hl, hc, w_in[layer], w_dn_conv_q[layer], w_dn_conv_kv[layer], dn_a_log[layer], dn_dt_bias[layer],
            q_norm[layer], k_norm[layer], cos2, sin2, ctx_out)
        merge_w = (dn_norm[layer], w_sc_conv[layer], mix_post[layer], ffn_pre[layer],
                   w_proj_dn[layer], w_proj_sc[layer], w_proj_attn[layer], w_out[layer])
        h_lat, f_l = merge(o_f, o_b, Lc, p_l, o_at_l, h_lat, lat_mod(layer, 2), lat_mod(layer, 3), lat_mod(layer, 4),
                           *merge_w)
        f = f_l.reshape(B * L, D)
        if ctx_out:
            h_ctx, f_c = merge(o_f, o_b, 0, p_c, o_at_c, h_ctx, ctx_mod(layer, 2), ctx_mod(layer, 3),
                               ctx_mod(layer, 4), *merge_w)
            f = jnp.concatenate([f, f_c.reshape(B * Lc, D)], axis=0)
        yg, wt = moe_routed(f, w_router[layer], b_router[layer], w_exp_gate[layer], w_exp_up[layer], w_exp_down[layer])
        shared_w = (w_sh_gate[layer], w_sh_up[layer], w_sh_down[layer])
        nxt_l = nxt_c = None
        if ctx_out:
            nxt_l = (mix_pre[layer + 1], lat_mod(layer + 1, 1), lat_mod(layer + 1, 0))
            nxt_c = (mix_pre[layer + 1], ctx_mod(layer + 1, 1), ctx_mod(layer + 1, 0))
            h_ctx, hc = combine(yg, wt, f, B * L, *shared_w, h_ctx, ctx_mod(layer, 5), ffn_post[layer], nxt_c)
        h_lat, hl = combine(yg, wt, f, 0, *shared_w, h_lat, lat_mod(layer, 5), ffn_post[layer], nxt_l)
    return h_lat


BATCH_STREAMS = 2


def kernel(x, c, ctx, c_ctx, w_mod, b_mod, mix_pre, mix_post, ffn_pre, ffn_post, w_in, w_dn_conv_q, w_dn_conv_kv, dn_a_log, dn_dt_bias, dn_norm, w_sc_conv, q_norm, k_norm, w_proj_dn, w_proj_sc, w_proj_attn, w_out, w_router, b_router, w_exp_gate, w_exp_up, w_exp_down, w_sh_gate, w_sh_up, w_sh_down):
    params = (w_mod, b_mod, mix_pre, mix_post, ffn_pre, ffn_post, w_in, w_dn_conv_q, w_dn_conv_kv, dn_a_log,
              dn_dt_bias, dn_norm, w_sc_conv, q_norm, k_norm, w_proj_dn, w_proj_sc, w_proj_attn, w_out, w_router,
              b_router, w_exp_gate, w_exp_up, w_exp_down, w_sh_gate, w_sh_up, w_sh_down)
    B = x.shape[0]
    n = BATCH_STREAMS if B % BATCH_STREAMS == 0 else 1
    step = B // n
    outs = [forward(x[i * step:(i + 1) * step], c[i * step:(i + 1) * step], ctx[i * step:(i + 1) * step], c_ctx, *params)
            for i in range(n)]
    return jnp.concatenate(outs, axis=0)
```
